```python
import math
import jax
import jax.numpy as jnp
from jax import lax
import numpy as np

D_MODEL = 1024
BATCH = 2
SEQ = 8192
DEPTH = 4

CTX_LEN = 256
GRID_W = 64

GROUP_WIDTH = D_MODEL // 4
N_IN_SLICES = 11
IN_DIM = N_IN_SLICES * GROUP_WIDTH

DA_HEADS = 4
DA_V_DIM = GROUP_WIDTH // DA_HEADS
DA_QK_DIM = DA_V_DIM // 2

RET_HEADS = 4
RET_DIM = GROUP_WIDTH // RET_HEADS
RET_CHUNK = 128

NA_HEADS = 4
NA_DIM = GROUP_WIDTH // NA_HEADS
NA_WIN_ROWS = 8
NA_WIN_COLS = 16

S5_CH = 16
S5_GROUPS = GROUP_WIDTH // S5_CH
S5_STATE = 64
S5_DT_MIN = 1e-3
S5_DT_MAX = 1e-1

FFN_DIM = ((8 * D_MODEL // 3 + 255) // 256) * 256
MOE_EXPERTS = 8
MOE_TOP_K = 2
MOE_FFN_DIM = 7 * D_MODEL // 2
N_DENSE = (DEPTH + 1) // 2
N_MOE = DEPTH // 2

ATTN_BLOCK = 128
ROPE_BASE = 10000.0
NORM_EPS = 1e-6

kernel_name = 'hybrid_head_group_diffusion_block'

F32 = jnp.float32


def _rms(x):
    xf = x.astype(F32)
    return xf * lax.rsqrt(jnp.mean(xf * xf, axis=-1, keepdims=True) + NORM_EPS)


def rms_norm(x, gain):
    return (_rms(x) * gain.astype(F32)).astype(x.dtype)


def modulate(h, shift, scale):
    return h * (1.0 + scale) + shift


def axial_rope_angles(length, dim):
    q = dim // 4
    t = jnp.arange(length)
    row = (t // GRID_W).astype(F32)
    col = (t % GRID_W).astype(F32)
    inv = ROPE_BASE ** (-jnp.arange(q, dtype=F32) / q)
    return jnp.stack([row[:, None] * inv, col[:, None] * inv], axis=1)


def apply_axial_rope(x, ang):
    q = x.shape[-1] // 4
    n_mid = x.ndim - 3
    ang = ang.reshape((ang.shape[0],) + (1,) * n_mid + (2, q))
    cos, sin = jnp.cos(ang), jnp.sin(ang)
    xr = x.reshape(x.shape[:-1] + (2, 2, q))
    x1, x2 = xr[..., 0, :], xr[..., 1, :]
    out = jnp.stack([x1 * cos - x2 * sin, x2 * cos + x1 * sin], axis=-2)
    return out.reshape(x.shape).astype(x.dtype)


def softmax_attention(q, k, v, scale):
    s = jnp.einsum('bqhd,bkhd->bhqk', q, k).astype(F32) * scale
    return jnp.einsum('bhqk,bkhd->bqhd', jax.nn.softmax(s, axis=-1), v)


def diff_attn_core(q, k, v, lam):
    s = jnp.einsum('bqhmd,bkhmd->bhmqk', q, k).astype(F32) * (DA_QK_DIM ** -0.5)
    p = jax.nn.softmax(s, axis=-1)
    w = p[:, :, 0] - lam * p[:, :, 1]
    return jnp.einsum('bhqk,bkhe->bqhe', w, v.astype(F32))


def diff_attention(qx, kx, vx, qc, kc, vc, ang, lam_p, gain, lam_init, with_ctx):
    B, L, _ = qx.shape
    qk = lambda t: t.reshape(t.shape[0], t.shape[1], DA_HEADS, 2, DA_QK_DIM)
    vh = lambda t: t.reshape(t.shape[0], t.shape[1], DA_HEADS, DA_V_DIM)
    lp = lam_p.astype(F32)
    lam = jnp.exp(jnp.sum(lp[0] * lp[1])) - jnp.exp(jnp.sum(lp[2] * lp[3])) + lam_init
    qx = apply_axial_rope(qk(qx), ang)
    kx = apply_axial_rope(qk(kx), ang)
    k_all = jnp.concatenate([kx, qk(kc)], axis=1)
    v_all = jnp.concatenate([vh(vx), vh(vc)], axis=1)
    n_blk = L // ATTN_BLOCK
    q_blocks = jnp.swapaxes(qx.reshape(B, n_blk, ATTN_BLOCK, DA_HEADS, 2, DA_QK_DIM), 0, 1)
    o = lax.map(lambda qb: diff_attn_core(qb, k_all, v_all, lam), q_blocks)
    o = jnp.swapaxes(o, 0, 1).reshape(B, L, DA_HEADS, DA_V_DIM)

    def finish(t):
        return (rms_norm(t, gain) * (1.0 - lam_init)).reshape(t.shape[0], t.shape[1], GROUP_WIDTH)

    out_c = finish(diff_attn_core(qk(qc), qk(kc), vh(vc), lam)) if with_ctx else None
    return finish(o), out_c


def retention_final_state(k, v, log_gamma):
    n = k.shape[1]
    age = (n - 1 - jnp.arange(n)).astype(F32)
    w = jnp.exp(log_gamma[:, None] * age[None, :])
    return jnp.einsum('bmhd,hm,bmhe->bhde', k, w, v)


def retention_chunkwise(q, k, v, log_gamma, s0):
    B, L, H, d = q.shape
    n, C = L // RET_CHUNK, RET_CHUNK
    qc, kc, vc = (t.reshape(B, n, C, H, d) for t in (q, k, v))
    pos = jnp.arange(C, dtype=F32)
    rel = pos[:, None] - pos[None, :]
    intra_decay = jnp.where(rel >= 0, jnp.exp(log_gamma[:, None, None] * jnp.maximum(rel, 0.0)), 0.0)
    s = jnp.einsum('bnihd,bnjhd->bnhij', qc, kc) * intra_decay
    intra = jnp.einsum('bnhij,bnjhe->bnihe', s, vc)
    k_w = jnp.exp(log_gamma[:, None] * (C - 1.0 - pos)[None, :])
    chunk_kv = jnp.einsum('bnjhd,hj,bnjhe->nbhde', kc, k_w, vc)
    chunk_decay = jnp.exp(log_gamma * C)[:, None, None]

    def step(state, kv):
        return chunk_decay * state + kv, state

    _, s_prev = lax.scan(step, s0.astype(F32), chunk_kv.astype(F32))
    q_w = jnp.exp(log_gamma[:, None] * (pos + 1.0)[None, :])
    cross = jnp.einsum('bnihd,hi,nbhde->bnihe', qc, q_w, s_prev)
    return (intra + cross).reshape(B, L, H, d)


def bidir_retention(qx, kx, vx, gx, qc, kc, vc, gc, ang, decay_logit, with_ctx):
    heads = lambda t: t.reshape(t.shape[0], t.shape[1], RET_HEADS, RET_DIM)
    flip = lambda t: jnp.flip(t, axis=1)
    scale = RET_DIM ** -0.5
    qx = apply_axial_rope(heads(qx), ang)
    kx = apply_axial_rope(heads(kx), ang) * scale
    vx = heads(vx)
    qc, kc, vc = heads(qc), heads(kc) * scale, heads(vc)
    log_g = jax.nn.log_sigmoid(decay_logit.astype(F32))
    s_fwd = retention_final_state(kc, vc, log_g[0])
    s_bwd = retention_final_state(flip(kc), flip(vc), log_g[1])
    ox = (retention_chunkwise(qx, kx, vx, log_g[0], s_fwd)
          + flip(retention_chunkwise(flip(qx), flip(kx), flip(vx), log_g[1], s_bwd)))

    def finish(o, g):
        return _rms(o).reshape(o.shape[0], o.shape[1], GROUP_WIDTH) * jax.nn.silu(g.astype(F32))

    out_c = None
    if with_ctx:
        zero = jnp.zeros(s_fwd.shape, F32)
        oc = (retention_chunkwise(qc, kc, vc, log_g[0], zero)
              + flip(retention_chunkwise(flip(qc), flip(kc), flip(vc), log_g[1], zero)))
        out_c = finish(oc, gc)
    return finish(ox, gx), out_c


def neighbourhood_attention(qx, kx, vx, qc, kc, vc, rpb, with_ctx):
    B, L, _ = qx.shape
    rows = L // GRID_W
    wr = min(NA_WIN_ROWS, rows)
    scale = NA_DIM ** -0.5
    heads = lambda t: t.reshape(t.shape[0], t.shape[1], NA_HEADS, NA_DIM)
    grid = lambda t: t.reshape(B, rows, GRID_W, NA_HEADS, NA_DIM)
    qg, kg, vg = grid(qx), grid(kx), grid(vx)
    qc, kc, vc = heads(qc), heads(kc), heads(vc)
    r = jnp.arange(rows)
    r0 = jnp.clip(r - wr // 2, 0, rows - wr)
    row_idx = r0[:, None] + jnp.arange(wr)[None, :]
    k_rows = kg[:, row_idx]
    v_rows = vg[:, row_idx]
    s_nb = jnp.einsum('brqhd,brikhd->bhrqik', qg, k_rows).astype(F32) * scale
    col = jnp.arange(GRID_W)
    c0 = jnp.clip(col - NA_WIN_COLS // 2, 0, GRID_W - NA_WIN_COLS)
    col_ok = (col[None, :] >= c0[:, None]) & (col[None, :] < c0[:, None] + NA_WIN_COLS)
    dr_idx = row_idx - r[:, None] + (NA_WIN_ROWS - 1)
    dc_idx = jnp.clip(col[None, :] - col[:, None] + (NA_WIN_COLS - 1), 0, 2 * NA_WIN_COLS - 2)
    bias = rpb.astype(F32)[:, dr_idx[:, None, :, None], dc_idx[None, :, None, :]]
    s_nb = jnp.where(col_ok[:, None, :], s_nb + bias, -jnp.inf)
    s_ctx = jnp.einsum('brqhd,bchd->bhrqc', qg, kc).astype(F32) * scale
    n_nb = wr * GRID_W
    p = jax.nn.softmax(jnp.concatenate([s_nb.reshape(B, NA_HEADS, rows, GRID_W, n_nb), s_ctx], axis=-1), axis=-1)
    p_nb = p[..., :n_nb].reshape(B, NA_HEADS, rows, GRID_W, wr, GRID_W)
    o = (jnp.einsum('bhrqik,brikhd->brqhd', p_nb, v_rows)
         + jnp.einsum('bhrqc,bchd->brqhd', p[..., n_nb:], vc))
    out_x = o.reshape(B, L, GROUP_WIDTH)
    out_c = softmax_attention(qc, kc, vc, scale).reshape(B, -1, GROUP_WIDTH) if with_ctx else None
    return out_x, out_c


def _ssm_combine(e1, e2):
    a1, b1 = e1
    a2, b2 = e2
    return a2 * a1, a2 * b1 + b2


def s5_scan(bu, a_bar, x0):
    bu = bu.at[:, 0].add(a_bar * x0)
    a = jnp.broadcast_to(a_bar, bu.shape)
    _, xs = lax.associative_scan(_ssm_combine, (a, bu), axis=1)
    return xs


def s5_final_state(bu, lam_dt):
    n = bu.shape[1]
    age = (n - 1 - jnp.arange(n)).astype(F32)
    w = jnp.exp(age[:, None, None] * lam_dt[None])
    return jnp.einsum('mgp,bmgp->bgp', w, bu)


def bidir_s5(ux, uc, a_re, a_im, b_re, b_im, c_re, c_im, log_step, d_skip, glu_w, glu_b, with_ctx):
    B, L, _ = ux.shape
    Lc = uc.shape[1]
    flip = lambda t: jnp.flip(t, axis=1)
    gx = ux.astype(F32).reshape(B, L, S5_GROUPS, S5_CH)
    gc = uc.astype(F32).reshape(B, Lc, S5_GROUPS, S5_CH)
    y_x = ux.astype(F32) * d_skip.astype(F32)
    y_c = uc.astype(F32) * d_skip.astype(F32)
    for direction in range(2):
        lam = lax.complex(a_re[direction].astype(F32), a_im[direction].astype(F32))
        lam_dt = lam * jnp.exp(log_step[direction].astype(F32))[:, None]
        a_bar = jnp.exp(lam_dt)
        b_mat = lax.complex(b_re[direction].astype(F32), b_im[direction].astype(F32))
        b_bar = ((a_bar - 1.0) / lam)[..., None] * b_mat
        c_mat = lax.complex(c_re[direction].astype(F32), c_im[direction].astype(F32))
        bu_x = jnp.einsum('blgh,gph->blgp', gx, b_bar)
        bu_c = jnp.einsum('blgh,gph->blgp', gc, b_bar)
        if direction == 1:
            bu_x, bu_c = flip(bu_x), flip(bu_c)
        xs = s5_scan(bu_x, a_bar, s5_final_state(bu_c, lam_dt))
        yd = jnp.real(jnp.einsum('ghp,blgp->blgh', c_mat, xs)).reshape(B, L, GROUP_WIDTH)
        y_x = y_x + (flip(yd) if direction == 1 else yd)
        if with_ctx:
            xs_c = s5_scan(bu_c, a_bar, jnp.zeros((B, S5_GROUPS, S5_STATE), bu_c.dtype))
            yc = jnp.real(jnp.einsum('ghp,blgp->blgh', c_mat, xs_c)).reshape(B, Lc, GROUP_WIDTH)
            y_c = y_c + (flip(yc) if direction == 1 else yc)

    def glu(y):
        act = jax.nn.gelu(y)
        return act * jax.nn.sigmoid(act @ glu_w.astype(F32) + glu_b.astype(F32))

    return glu(y_x), (glu(y_c) if with_ctx else None)


def swiglu(h, w_gate, w_up, w_down):
    return (jax.nn.silu(h @ w_gate) * (h @ w_up)) @ w_down


def moe_swiglu(h, router, w_gate, w_up, w_down):
    logits = (h @ router).astype(F32)
    top_v, top_i = lax.top_k(logits, MOE_TOP_K)
    top_w = jax.nn.softmax(top_v, axis=-1)
    gates = jnp.sum(jax.nn.one_hot(top_i, MOE_EXPERTS, dtype=F32) * top_w[..., None], axis=-2)
    y = jnp.zeros(h.shape, F32)
    for e in range(MOE_EXPERTS):
        y = y + gates[..., e:e + 1] * swiglu(h, w_gate[e], w_up[e], w_down[e])
    return y.astype(h.dtype)


def setup_inputs(seed: int = 0) -> dict:
    key = jax.random.key(seed)
    keys = iter(jax.random.split(key, 40))

    def normal(shape, scale):
        return jax.random.normal(next(keys), shape, F32) * scale

    D = D_MODEL
    ret_logit = jnp.log(2.0 ** (5.0 + jnp.arange(RET_HEADS, dtype=F32)) - 1.0)
    s5_shape = (DEPTH, 2, S5_GROUPS, S5_STATE)
    return {
        'x': normal((BATCH, SEQ, D), 1.0),
        'c': normal((BATCH, D), 1.0),
        'ctx': normal((BATCH, CTX_LEN, D), 1.0),
        'c_ctx': normal((D,), 1.0),
        'w_mod': normal((DEPTH, D, 6 * D), 0.5 * D ** -0.5),
        'b_mod': normal((DEPTH, 6 * D), 0.01),
        'norm_gain': 1.0 + normal((DEPTH, 4, D), 0.05),
        'w_in': normal((DEPTH, D, IN_DIM), D ** -0.5),
        'w_out': normal((DEPTH, D, D), D ** -0.5),
        'da_lambda': normal((DEPTH, 4, DA_QK_DIM), 0.1),
        'da_norm_gain': 1.0 + normal((DEPTH, DA_V_DIM), 0.05),
        'ret_decay_logit': ret_logit + normal((DEPTH, 2, RET_HEADS), 0.05),
        'na_rpb': normal((DEPTH, NA_HEADS, 2 * NA_WIN_ROWS - 1, 2 * NA_WIN_COLS - 1), 0.1),
        's5_a_re': -0.5 + normal(s5_shape, 0.01),
        's5_a_im': math.pi * jnp.arange(S5_STATE, dtype=F32) + normal(s5_shape, 0.01),
        's5_b_re': normal((DEPTH, 2, S5_GROUPS, S5_STATE, S5_CH), (2 * S5_CH) ** -0.5),
        's5_b_im': normal((DEPTH, 2, S5_GROUPS, S5_STATE, S5_CH), (2 * S5_CH) ** -0.5),
        's5_c_re': normal((DEPTH, 2, S5_GROUPS, S5_CH, S5_STATE), S5_STATE ** -0.5),
        's5_c_im': normal((DEPTH, 2, S5_GROUPS, S5_CH, S5_STATE), S5_STATE ** -0.5),
        's5_log_step': jax.random.uniform(next(keys), (DEPTH, 2, S5_GROUPS), F32,
                                          math.log(S5_DT_MIN), math.log(S5_DT_MAX)),
        's5_d': normal((DEPTH, GROUP_WIDTH), 1.0),
        's5_glu_w': normal((DEPTH, GROUP_WIDTH, GROUP_WIDTH), GROUP_WIDTH ** -0.5),
        's5_glu_b': normal((DEPTH, GROUP_WIDTH), 0.01),
        'ffn_w_gate': normal((N_DENSE, D, FFN_DIM), D ** -0.5),
        'ffn_w_up': normal((N_DENSE, D, FFN_DIM), D ** -0.5),
        'ffn_w_down': normal((N_DENSE, FFN_DIM, D), FFN_DIM ** -0.5),
        'moe_router': normal((N_MOE, D, MOE_EXPERTS), D ** -0.5),
        'moe_w_gate': normal((N_MOE, MOE_EXPERTS, D, MOE_FFN_DIM), D ** -0.5),
        'moe_w_up': normal((N_MOE, MOE_EXPERTS, D, MOE_FFN_DIM), D ** -0.5),
        'moe_w_down': normal((N_MOE, MOE_EXPERTS, MOE_FFN_DIM, D), MOE_FFN_DIM ** -0.5),
    }


def reference(x, c, ctx, c_ctx, w_mod, b_mod, norm_gain, w_in, w_out, da_lambda, da_norm_gain,
              ret_decay_logit, na_rpb, s5_a_re, s5_a_im, s5_b_re, s5_b_im, s5_c_re, s5_c_im,
              s5_log_step, s5_d, s5_glu_w, s5_glu_b, ffn_w_gate, ffn_w_up, ffn_w_down,
              moe_router, moe_w_gate, moe_w_up, moe_w_down):
    L = x.shape[1]
    ang_da = axial_rope_angles(L, DA_QK_DIM)
    ang_ret = axial_rope_angles(L, RET_DIM)
    silu_c = jax.nn.silu(c)
    silu_cc = jax.nn.silu(c_ctx)
    h_ctx = ctx
    for layer in range(DEPTH):
        with_ctx = layer < DEPTH - 1
        mx = [m[:, None, :] for m in jnp.split(silu_c @ w_mod[layer] + b_mod[layer], 6, axis=-1)]
        mc = jnp.split(silu_cc @ w_mod[layer] + b_mod[layer], 6, axis=-1)
        g = norm_gain[layer]

        hx = modulate(rms_norm(x, g[0]), mx[0], mx[1])
        hc = modulate(rms_norm(h_ctx, g[0]), mc[0], mc[1])
        zx = jnp.split(hx @ w_in[layer], N_IN_SLICES, axis=-1)
        zc = jnp.split(hc @ w_in[layer], N_IN_SLICES, axis=-1)
        lam_init = 0.8 - 0.6 * math.exp(-0.3 * layer)
        a_x, a_c = diff_attention(zx[0], zx[1], zx[2], zc[0], zc[1], zc[2], ang_da,
                                  da_lambda[layer], da_norm_gain[layer], lam_init, with_ctx)
        r_x, r_c = bidir_retention(zx[3], zx[4], zx[5], zx[6], zc[3], zc[4], zc[5], zc[6],
                                   ang_ret, ret_decay_logit[layer], with_ctx)
        n_x, n_c = neighbourhood_attention(zx[7], zx[8], zx[9], zc[7], zc[8], zc[9],
                                           na_rpb[layer], with_ctx)
        s_x, s_c = bidir_s5(zx[10], zc[10], s5_a_re[layer], s5_a_im[layer], s5_b_re[layer],
                            s5_b_im[layer], s5_c_re[layer], s5_c_im[layer], s5_log_step[layer],
                            s5_d[layer], s5_glu_w[layer], s5_glu_b[layer], with_ctx)
        mix_x = jnp.concatenate([a_x, r_x, n_x, s_x], axis=-1).astype(x.dtype) @ w_out[layer]
        x = x + mx[2] * rms_norm(mix_x, g[1])
        if with_ctx:
            mix_c = jnp.concatenate([a_c, r_c, n_c, s_c], axis=-1).astype(h_ctx.dtype) @ w_out[layer]
            h_ctx = h_ctx + mc[2] * rms_norm(mix_c, g[1])

        f_in = modulate(rms_norm(x, g[2]), mx[3], mx[4])
        if with_ctx:
            f_in = jnp.concatenate([f_in, modulate(rms_norm(h_ctx, g[2]), mc[3], mc[4])], axis=1)
        i = layer // 2
        if layer % 2 == 0:
            f_out = swiglu(f_in, ffn_w_gate[i], ffn_w_up[i], ffn_w_down[i])
        else:
            f_out = moe_swiglu(f_in, moe_router[i], moe_w_gate[i], moe_w_up[i], moe_w_down[i])
        x = x + mx[5] * rms_norm(f_out[:, :L], g[3])
        if with_ctx:
            h_ctx = h_ctx + mc[5] * rms_norm(f_out[:, L:], g[3])
    return x
```

```python
import functools
import math

import numpy as np
import jax
import jax.numpy as jnp
from jax import lax
from jax.experimental import pallas as pl
from jax.experimental.pallas import tpu as pltpu

F32 = jnp.float32
BF16 = jnp.bfloat16

GRID_W = 64
GROUP = 256
N_IN_SLICES = 11
DA_HEADS, DA_QK_DIM, DA_V_DIM = 4, 32, 64
RET_HEADS, RET_DIM = 4, 64
NA_HEADS, NA_DIM, NA_WIN_ROWS, NA_WIN_COLS = 4, 64, 8, 16
S5_CH, S5_GROUPS, S5_STATE = 16, 16, 64
MOE_EXPERTS, MOE_TOP_K = 8, 2
ROPE_BASE = 10000.0
NORM_EPS = 1e-6
NEG_BIG = -1e30

S5_CHUNK = 8
S5_ROW = S5_CHUNK * GROUP
S5_NSTATE = S5_GROUPS * S5_STATE
ATT_BLOCK = 256
VMEM_LIMIT = 48 * 1024 * 1024


def _cparams(n_axes):
    return pltpu.CompilerParams(dimension_semantics=("arbitrary",) * n_axes,
                                vmem_limit_bytes=VMEM_LIMIT)


def _pick_tile(n, target, mult=16):
    best = None
    for t in range(mult, min(n, target) + 1, mult):
        if n % t == 0:
            best = t
    assert best is not None, (n, target)
    return best


def _rms_rows(x):
    return x * lax.rsqrt(jnp.mean(x * x, axis=-1, keepdims=True) + NORM_EPS)


def _row_mod(mod_ref, k, m, tm, n_lat, d):
    row = m * tm + lax.broadcasted_iota(jnp.int32, (tm, 1), 0)
    return jnp.where(row >= n_lat, mod_ref[1:2, k * d:(k + 1) * d], mod_ref[0:1, k * d:(k + 1) * d])


def _mod_kernel(c_ref, w_ref, b_ref, o_ref):
    o_ref[...] = jnp.dot(jax.nn.silu(c_ref[...]), w_ref[...], precision=lax.Precision.HIGHEST,
                         preferred_element_type=F32) + b_ref[...]


def _modulation(c_rows, w_mod, b_mod):
    depth, d, n6 = w_mod.shape
    tn = 1024
    return pl.pallas_call(
        _mod_kernel,
        grid=(depth, n6 // tn),
        in_specs=[pl.BlockSpec((8, d), lambda l, n: (0, 0)),
                  pl.BlockSpec((None, d, tn), lambda l, n: (l, 0, n)),
                  pl.BlockSpec((None, 1, tn), lambda l, n: (l, 0, n))],
        out_specs=pl.BlockSpec((None, 8, tn), lambda l, n: (l, 0, n)),
        out_shape=jax.ShapeDtypeStruct((depth, 8, n6), F32),
        compiler_params=_cparams(2),
    )(c_rows, w_mod, b_mod.reshape(depth, 1, n6))


def _rope(z, cos, sin_signed, q):
    n = z.shape[-1]
    lane = lax.broadcasted_iota(jnp.int32, z.shape, 1)
    first = (lane % (2 * q)) < q
    partner = jnp.where(first, pltpu.roll(z, n - q, 1), pltpu.roll(z, q, 1))
    return z * cos + partner * sin_signed


def _in_proj_kernel(x_ref, mod_ref, gain_ref, w_ref, cda_ref, sda_ref, crt_ref, srt_ref, z_ref,
                    *, n_lat, tm, d):
    m = pl.program_id(1)
    shift = _row_mod(mod_ref, 0, m, tm, n_lat, d)
    scale = _row_mod(mod_ref, 1, m, tm, n_lat, d)
    h = ((_rms_rows(x_ref[...]) * gain_ref[...]) * (1.0 + scale) + shift).astype(BF16)
    col_scale = {0: DA_QK_DIM ** -0.5, 4: RET_DIM ** -0.5, 7: NA_DIM ** -0.5}
    for j in range(N_IN_SLICES):
        zj = jnp.dot(h, w_ref[:, j * GROUP:(j + 1) * GROUP], preferred_element_type=F32)
        if j in (0, 1):
            zj = _rope(zj, cda_ref[...], sda_ref[...], DA_QK_DIM // 4)
        elif j in (3, 4):
            zj = _rope(zj, crt_ref[...], srt_ref[...], RET_DIM // 4)
        if j in col_scale:
            zj = zj * col_scale[j]
        z_ref[:, j * GROUP:(j + 1) * GROUP] = zj.astype(BF16)


def _in_proj(x, mod, gain, w_bf, tabs, n_lat):
    b, t, d = x.shape
    n_out = w_bf.shape[1]
    tm = _pick_tile(t, 528)
    tab_spec = pl.BlockSpec((tm, GROUP), lambda bi, m: (m, 0))
    return pl.pallas_call(
        functools.partial(_in_proj_kernel, n_lat=n_lat, tm=tm, d=d),
        grid=(b, t // tm),
        in_specs=[pl.BlockSpec((None, tm, d), lambda bi, m: (bi, m, 0)),
                  pl.BlockSpec((None, 2, mod.shape[-1]), lambda bi, m: (bi, 0, 0)),
                  pl.BlockSpec((1, d), lambda bi, m: (0, 0)),
                  pl.BlockSpec((d, n_out), lambda bi, m: (0, 0)),
                  tab_spec, tab_spec, tab_spec, tab_spec],
        out_specs=pl.BlockSpec((None, tm, n_out), lambda bi, m: (bi, m, 0)),
        out_shape=jax.ShapeDtypeStruct((b, t, n_out), BF16),
        compiler_params=_cparams(2),
    )(x, mod, gain, w_bf, *tabs)


def _rope_tables(n_lat, t, dim):
    q = dim // 4
    tok = jnp.arange(n_lat)
    row = (tok // GRID_W).astype(F32)
    col = (tok % GRID_W).astype(F32)
    inv = ROPE_BASE ** (-jnp.arange(q, dtype=F32) / q)
    lane = np.arange(GROUP) % dim
    axis, half, qi = lane // (2 * q), (lane % (2 * q)) // q, lane % q
    pos = jnp.where(jnp.asarray(axis)[None, :] == 0, row[:, None], col[:, None])
    ang = pos * inv[jnp.asarray(qi)][None, :]
    sign = jnp.asarray(np.where(half == 0, -1.0, 1.0), F32)[None, :]
    cos = jnp.concatenate([jnp.cos(ang), jnp.ones((t - n_lat, GROUP), F32)], axis=0)
    sin = jnp.concatenate([jnp.sin(ang) * sign, jnp.zeros((t - n_lat, GROUP), F32)], axis=0)
    return cos, sin


def _diff_attn_kernel(lam_ref, q_ref, kt_ref, v_ref, gain_ref, o_ref, m_scr, l_scr, acc_scr,
                      *, n_lat, t, tq, tk, fin_scale):
    qi = pl.program_id(1)
    n_chunks = t // tk
    c_lo = jnp.where(qi * tq >= n_lat, n_lat // tk, 0)
    m_scr[...] = jnp.full(m_scr.shape, -jnp.inf, F32)
    l_scr[...] = jnp.zeros(l_scr.shape, F32)
    acc_scr[...] = jnp.zeros(acc_scr.shape, F32)
    q = q_ref[...]
    n_maps = 2 * DA_HEADS
    qs = [q[:, i * DA_QK_DIM:(i + 1) * DA_QK_DIM] for i in range(n_maps)]

    def body(c, carry):
        off = pl.multiple_of(c * tk, tk)
        for i in range(n_maps):
            pair = (i // 2) // 2
            kt = kt_ref[i * DA_QK_DIM:(i + 1) * DA_QK_DIM, pl.ds(off, tk)]
            s = jnp.dot(qs[i], kt, preferred_element_type=F32)
            m_prev = m_scr[i]
            m_next = jnp.maximum(m_prev, jnp.max(s, axis=1, keepdims=True))
            p = jnp.exp(s - m_next[:, 0:1])
            alpha = jnp.exp(m_prev - m_next)
            l_scr[i] = alpha * l_scr[i] + jnp.sum(p, axis=1, keepdims=True)
            vb = v_ref[pl.ds(off, tk), pair * 128:(pair + 1) * 128]
            acc_scr[i] = alpha * acc_scr[i] + jnp.dot(p.astype(BF16), vb, preferred_element_type=F32)
            m_scr[i] = m_next
        return carry

    lax.fori_loop(c_lo, n_chunks, body, 0)
    lam = lam_ref[0]
    outs = []
    for h in range(DA_HEADS):
        o = acc_scr[2 * h] / l_scr[2 * h] - lam * (acc_scr[2 * h + 1] / l_scr[2 * h + 1])
        oh = o[:, (h % 2) * DA_V_DIM:(h % 2 + 1) * DA_V_DIM]
        outs.append(_rms_rows(oh) * gain_ref[...] * fin_scale)
    o_ref[...] = jnp.concatenate(outs, axis=-1).astype(BF16)


def _diff_attention(z, kt, lam, gain, n_lat, fin_scale):
    b, t, _ = z.shape
    tq = tk = ATT_BLOCK
    n_maps = 2 * DA_HEADS
    return pl.pallas_call(
        functools.partial(_diff_attn_kernel, n_lat=n_lat, t=t, tq=tq, tk=tk, fin_scale=fin_scale),
        grid=(b, t // tq),
        in_specs=[pl.BlockSpec(memory_space=pltpu.SMEM),
                  pl.BlockSpec((None, tq, GROUP), lambda bi, m: (bi, m, 0)),
                  pl.BlockSpec((None, GROUP, t), lambda bi, m: (bi, 0, 0)),
                  pl.BlockSpec((None, t, GROUP), lambda bi, m: (bi, 0, 2)),
                  pl.BlockSpec((1, DA_V_DIM), lambda bi, m: (0, 0))],
        out_specs=pl.BlockSpec((None, tq, GROUP), lambda bi, m: (bi, m, 0)),
        out_shape=jax.ShapeDtypeStruct((b, t, GROUP), BF16),
        scratch_shapes=[pltpu.VMEM((n_maps, tq, 128), F32)] * 3,
        compiler_params=_cparams(2),
    )(lam, z, kt, z, gain)


def _retention_kernel(q_ref, k_ref, v_ref, g_ref, dm_ref, qw_ref, kw_ref, cd_ref, bd_ref, o_ref,
                      st_scr, of_scr, *, c, n_chunks, n_lat_chunks):
    d = pl.program_id(1)
    i = pl.program_id(2)
    chunk = jnp.where(d == 0, (i + n_lat_chunks) % n_chunks, n_chunks - 1 - i)

    @pl.when(i == 0)
    def _():
        st_scr[...] = jnp.zeros(st_scr.shape, F32)

    q, k, v = q_ref[...], k_ref[...], v_ref[...]
    lane_head = lax.broadcasted_iota(jnp.int32, (1, GROUP), 1) // RET_DIM
    o = jnp.dot((q.astype(F32) * qw_ref[...]).astype(BF16), st_scr[...].astype(BF16),
                preferred_element_type=F32)
    for h in range(RET_HEADS):
        mh = lane_head == h
        qh = q * mh.astype(BF16)
        s = lax.dot_general(qh, k, (((1,), (1,)), ((), ())), preferred_element_type=F32) * dm_ref[h]
        o = o + jnp.where(mh, jnp.dot(s.astype(BF16), v, preferred_element_type=F32), 0.0)
    kwk = (k.astype(F32) * kw_ref[...]).astype(BF16)
    kv = lax.dot_general(kwk, v, (((0,), (0,)), ((), ())), preferred_element_type=F32)
    st_scr[...] = cd_ref[...] * st_scr[...] + bd_ref[...] * kv
    off = pl.multiple_of(chunk * c, c)

    @pl.when(d == 0)
    def _():
        of_scr[pl.ds(off, c), :] = o

    @pl.when(d == 1)
    def _():
        tot = o + of_scr[pl.ds(off, c), :]
        sq = tot * tot
        inv = jnp.zeros(tot.shape, F32)
        for h in range(RET_HEADS):
            mh = lane_head == h
            ms = jnp.sum(jnp.where(mh, sq, 0.0), axis=-1, keepdims=True) * (1.0 / RET_DIM)
            inv = jnp.where(mh, lax.rsqrt(ms + NORM_EPS), inv)
        o_ref[...] = (tot * inv * jax.nn.silu(g_ref[...].astype(F32))).astype(BF16)


def _retention_tables(decay_logit, c):
    log_g = jax.nn.log_sigmoid(decay_logit.astype(F32))
    pos = jnp.arange(c, dtype=F32)
    rel = pos[:, None] - pos[None, :]
    rel = jnp.stack([rel, -rel])
    dm = jnp.where(rel[:, None] >= 0, jnp.exp(log_g[:, :, None, None] * jnp.maximum(rel[:, None], 0.0)), 0.0)
    lg_lane = jnp.repeat(log_g, RET_DIM, axis=1)
    q_exp = jnp.stack([pos + 1.0, c - pos])
    k_exp = jnp.stack([c - 1.0 - pos, pos])
    qw = jnp.exp(lg_lane[:, None, :] * q_exp[:, :, None])
    kw = jnp.exp(lg_lane[:, None, :] * k_exp[:, :, None])
    head = np.arange(GROUP) // RET_DIM
    bd = jnp.asarray((head[:, None] == head[None, :]).astype(np.float32))
    cd = jnp.exp(lg_lane * c)[:, :, None] * jnp.ones((1, 1, GROUP), F32)
    return dm, qw, kw, cd, bd


def _retention(z, decay_logit, n_lat):
    b, t, _ = z.shape
    c = ATT_BLOCK
    n_chunks, n_lat_chunks = t // c, n_lat // c
    dm, qw, kw, cd, bd = _retention_tables(decay_logit, c)

    def chunk_of(d, i):
        return jnp.where(d == 0, (i + n_lat_chunks) % n_chunks, n_chunks - 1 - i)

    def zspec(col):
        return pl.BlockSpec((None, c, GROUP), lambda bi, d, i: (bi, chunk_of(d, i), col))

    return pl.pallas_call(
        functools.partial(_retention_kernel, c=c, n_chunks=n_chunks, n_lat_chunks=n_lat_chunks),
        grid=(b, 2, n_chunks),
        in_specs=[zspec(3), zspec(4), zspec(5), zspec(6),
                  pl.BlockSpec((None, RET_HEADS, c, c), lambda bi, d, i: (d, 0, 0, 0)),
                  pl.BlockSpec((None, c, GROUP), lambda bi, d, i: (d, 0, 0)),
                  pl.BlockSpec((None, c, GROUP), lambda bi, d, i: (d, 0, 0)),
                  pl.BlockSpec((None, GROUP, GROUP), lambda bi, d, i: (d, 0, 0)),
                  pl.BlockSpec((GROUP, GROUP), lambda bi, d, i: (0, 0))],
        out_specs=pl.BlockSpec((None, c, GROUP),
                               lambda bi, d, i: (bi, jnp.where(d == 0, n_chunks - 1, n_chunks - 1 - i), 0)),
        out_shape=jax.ShapeDtypeStruct((b, t, GROUP), BF16),
        scratch_shapes=[pltpu.VMEM((GROUP, GROUP), F32), pltpu.VMEM((t, GROUP), F32)],
        compiler_params=_cparams(3),
    )(z, z, z, z, dm, qw, kw, cd, bd)


def _nbr_attn_kernel(q_ref, kp_ref, kc_ref, kn_ref, kx_ref, vp_ref, vc_ref, vn_ref, vx_ref, tab_ref, o_ref):
    q = q_ref[...]
    k = jnp.concatenate([kp_ref[...], kc_ref[...], kn_ref[...], kx_ref[...]], axis=0)
    v = jnp.concatenate([vp_ref[...], vc_ref[...], vn_ref[...], vx_ref[...]], axis=0)
    lane_head = lax.broadcasted_iota(jnp.int32, (1, GROUP), 1) // NA_DIM
    o = jnp.zeros(q.shape, F32)
    for h in range(NA_HEADS):
        mh = lane_head == h
        qh = q * mh.astype(BF16)
        s = lax.dot_general(qh, k, (((1,), (1,)), ((), ())), preferred_element_type=F32) + tab_ref[h]
        p = jnp.exp(s - jnp.max(s, axis=-1, keepdims=True))
        l = jnp.sum(p, axis=-1, keepdims=True)
        o = o + jnp.where(mh, jnp.dot(p.astype(BF16), v, preferred_element_type=F32) / l, 0.0)
    o_ref[...] = o.astype(BF16)


def _nbr_tables(rpb, n_lat, n_ctx):
    blk_rows = ATT_BLOCK // GRID_W
    rows = n_lat // GRID_W
    n_blk = rows // blk_rows
    assert rows >= NA_WIN_ROWS and n_blk >= 3
    ql = np.arange(ATT_BLOCK)
    kl = np.arange(3 * ATT_BLOCK)
    qcol, kcol = ql % GRID_W, kl % GRID_W
    c0 = np.clip(qcol - NA_WIN_COLS // 2, 0, GRID_W - NA_WIN_COLS)
    col_ok = (kcol[None, :] >= c0[:, None]) & (kcol[None, :] < c0[:, None] + NA_WIN_COLS)
    dc = np.clip(kcol[None, :] - qcol[:, None] + (NA_WIN_COLS - 1), 0, 2 * NA_WIN_COLS - 2)
    rpb = rpb.astype(F32)
    tabs = []
    for m in (0, 1, n_blk - 1):
        qrow = blk_rows * m + ql // GRID_W
        krow = blk_rows * (m - 1) + kl // GRID_W
        r0 = np.clip(qrow - NA_WIN_ROWS // 2, 0, rows - NA_WIN_ROWS)
        row_ok = (krow[None, :] >= r0[:, None]) & (krow[None, :] < r0[:, None] + NA_WIN_ROWS)
        dr = np.clip(krow[None, :] - qrow[:, None] + (NA_WIN_ROWS - 1), 0, 2 * NA_WIN_ROWS - 2)
        bias = rpb[:, jnp.asarray(dr), jnp.asarray(dc)]
        tabs.append(jnp.where(jnp.asarray(row_ok & col_ok)[None], bias, NEG_BIG))
    tabs.append(jnp.full((NA_HEADS, ATT_BLOCK, 3 * ATT_BLOCK), NEG_BIG, F32))
    tab = jnp.stack(tabs)
    return jnp.concatenate([tab, jnp.zeros((4, NA_HEADS, ATT_BLOCK, n_ctx), F32)], axis=-1)


def _nbr_attention(z, rpb, n_lat):
    b, t, _ = z.shape
    n_ctx = t - n_lat
    assert n_ctx == ATT_BLOCK and n_lat % ATT_BLOCK == 0
    n_blk = n_lat // ATT_BLOCK
    tab = _nbr_tables(rpb, n_lat, n_ctx)

    def kv_specs(col):
        mk = lambda f: pl.BlockSpec((None, ATT_BLOCK, GROUP), lambda bi, m: (bi, f(m), col))
        return [mk(lambda m: jnp.clip(m - 1, 0, n_blk - 1)), mk(lambda m: jnp.minimum(m, n_blk - 1)),
                mk(lambda m: jnp.clip(m + 1, 0, n_blk - 1)), mk(lambda m: n_blk)]

    def variant(m):
        return jnp.where(m == 0, 0, jnp.where(m == n_blk - 1, 2, jnp.where(m == n_blk, 3, 1)))

    win = 3 * ATT_BLOCK + n_ctx
    return pl.pallas_call(
        _nbr_attn_kernel,
        grid=(b, n_blk + 1),
        in_specs=[pl.BlockSpec((None, ATT_BLOCK, GROUP), lambda bi, m: (bi, m, 7))] + kv_specs(8) + kv_specs(9)
                 + [pl.BlockSpec((None, NA_HEADS, ATT_BLOCK, win), lambda bi, m: (variant(m), 0, 0, 0))],
        out_specs=pl.BlockSpec((None, ATT_BLOCK, GROUP), lambda bi, m: (bi, m, 0)),
        out_shape=jax.ShapeDtypeStruct((b, t, GROUP), BF16),
        compiler_params=_cparams(2),
    )(*([z] * 9), tab)


def _s5_kernel(u_ref, x0_ref, mm_ref, sm_ref, rm_ref, ap_ref, y_ref, xf_ref, st_scr, sloc_scr, xin_scr,
               *, rb):
    d = pl.program_id(0)
    i = pl.program_id(2)

    @pl.when(i == 0)
    def _():
        st_scr[...] = x0_ref[...]

    u = u_ref[...]
    sloc_scr[...] = jnp.dot(u, sm_ref[...], preferred_element_type=F32)
    ar, ai = ap_ref[0:1, :], ap_ref[1:2, :]
    ns = S5_NSTATE

    def step(n, carry):
        xr, xi = carry
        r = jnp.where(d == 0, n, rb - 1 - n)
        xin_scr[pl.ds(r, 1), 0:ns] = xr
        xin_scr[pl.ds(r, 1), ns:2 * ns] = xi
        s = sloc_scr[pl.ds(r, 1), :]
        return ar * xr - ai * xi + s[:, 0:ns], ar * xi + ai * xr + s[:, ns:2 * ns]

    xr, xi = lax.fori_loop(0, rb, step, (st_scr[:, 0:ns], st_scr[:, ns:2 * ns]))
    st_scr[:, 0:ns] = xr
    st_scr[:, ns:2 * ns] = xi
    y_ref[...] = (jnp.dot(u, mm_ref[...], preferred_element_type=F32)
                  + jnp.dot(xin_scr[...].astype(BF16), rm_ref[...], preferred_element_type=F32))
    xf_ref[...] = st_scr[...]


def _s5_operators(a_re, a_im, b_re, b_im, c_re, c_im, log_step):
    tc, g, p, ch = S5_CHUNK, S5_GROUPS, S5_STATE, S5_CH
    lam = lax.complex(a_re.astype(F32), a_im.astype(F32))
    lam_dt = lam * jnp.exp(log_step.astype(F32))[..., None]
    a_bar = jnp.exp(lam_dt)
    b_bar = ((a_bar - 1.0) / lam)[..., None] * lax.complex(b_re.astype(F32), b_im.astype(F32))
    c_mat = lax.complex(c_re.astype(F32), c_im.astype(F32))
    steps = jnp.arange(tc + 1, dtype=F32)
    apw = jnp.exp(lam_dt[:, None] * steps[None, :, None, None])
    eye = jnp.eye(g, dtype=F32)
    idx = np.arange(tc)
    outs = []
    for d in range(2):
        kern = jnp.real(jnp.einsum('gcp,tgp,gph->tghc', c_mat[d], apw[d], b_bar[d]))
        lag = (idx[None, :] - idx[:, None]) if d == 0 else (idx[:, None] - idx[None, :])
        kt = jnp.where(jnp.asarray(lag >= 0)[:, :, None, None, None], kern[jnp.asarray(np.maximum(lag, 0))], 0.0)
        mm = jnp.einsum('jighc,gk->jghikc', kt, eye).reshape(tc * g * ch, tc * g * ch)
        f_exp = (tc - 1 - idx) if d == 0 else idx
        ab = apw[d][jnp.asarray(f_exp)][..., None] * b_bar[d][None]
        sm = jnp.concatenate([jnp.einsum('jgph,gk->jghkp', part, eye).reshape(tc * g * ch, g * p)
                              for part in (jnp.real(ab), jnp.imag(ab))], axis=1)
        e_exp = (idx + 1) if d == 0 else (tc - idx)
        ca = c_mat[d][None] * apw[d][jnp.asarray(e_exp)][:, :, None, :]
        rm = jnp.concatenate([jnp.einsum('igcp,gk->gpikc', part, eye).reshape(g * p, tc * g * ch)
                              for part in (jnp.real(ca), -jnp.imag(ca))], axis=0)
        ap = jnp.stack([jnp.real(apw[d][tc]).reshape(-1), jnp.imag(apw[d][tc]).reshape(-1)])
        outs.append((mm, sm, rm, ap))
    mm, sm, rm, ap = (jnp.stack(v) for v in zip(*outs))
    return mm.astype(BF16), sm.astype(BF16), rm.astype(BF16), ap


def _s5_scan(u_rows, x0, ops, rb):
    b, r, _ = u_rows.shape
    nb = r // rb
    mm, sm, rm, ap = ops
    blk = lambda d, i: jnp.where(d == 0, i, nb - 1 - i)
    op_spec = lambda shp: pl.BlockSpec((None,) + shp, lambda d, bi, i: (d, 0, 0), pipeline_mode=pl.Buffered(1))
    return pl.pallas_call(
        functools.partial(_s5_kernel, rb=rb),
        grid=(2, b, nb),
        in_specs=[pl.BlockSpec((None, rb, S5_ROW), lambda d, bi, i: (bi, blk(d, i), 0)),
                  pl.BlockSpec((None, None, 1, 2 * S5_NSTATE), lambda d, bi, i: (d, bi, 0, 0)),
                  op_spec((S5_ROW, S5_ROW)), op_spec((S5_ROW, 2 * S5_NSTATE)),
                  op_spec((2 * S5_NSTATE, S5_ROW)),
                  pl.BlockSpec((None, 2, S5_NSTATE), lambda d, bi, i: (d, 0, 0))],
        out_specs=[pl.BlockSpec((None, None, rb, S5_ROW), lambda d, bi, i: (d, bi, blk(d, i), 0)),
                   pl.BlockSpec((None, None, 1, 2 * S5_NSTATE), lambda d, bi, i: (d, bi, 0, 0))],
        out_shape=[jax.ShapeDtypeStruct((2, b, r, S5_ROW), F32),
                   jax.ShapeDtypeStruct((2, b, 1, 2 * S5_NSTATE), F32)],
        scratch_shapes=[pltpu.VMEM((1, 2 * S5_NSTATE), F32), pltpu.VMEM((rb, 2 * S5_NSTATE), F32),
                        pltpu.VMEM((rb, 2 * S5_NSTATE), F32)],
        compiler_params=_cparams(3),
    )(u_rows, x0, mm, sm, rm, ap)


def _s5_glu_kernel(y_ref, u_ref, d_ref, w_ref, b_ref, o_ref):
    y = u_ref[...].astype(F32) * d_ref[...] + y_ref[0] + y_ref[1]
    act = jax.nn.gelu(y)
    gate = jnp.dot(act.astype(BF16), w_ref[...], preferred_element_type=F32) + b_ref[...]
    o_ref[...] = (act * jax.nn.sigmoid(gate)).astype(BF16)


def _s5_mixer(z, ops, d_skip, glu_w, glu_b, n_lat):
    b, t, _ = z.shape
    u = z[:, :, 10 * GROUP:11 * GROUP]
    u_lat = u[:, :n_lat].reshape(b, n_lat // S5_CHUNK, S5_ROW)
    u_ctx = u[:, n_lat:].reshape(b, (t - n_lat) // S5_CHUNK, S5_ROW)
    zero = jnp.zeros((2, b, 1, 2 * S5_NSTATE), F32)
    y_ctx, x_ctx = _s5_scan(u_ctx, zero, ops, u_ctx.shape[1])
    y_lat, _ = _s5_scan(u_lat, x_ctx, ops, _pick_tile(u_lat.shape[1], 256))
    y = jnp.concatenate([y_lat, y_ctx], axis=2).reshape(2, b, t, GROUP)
    tm = _pick_tile(t, 1056)
    return pl.pallas_call(
        _s5_glu_kernel,
        grid=(b, t // tm),
        in_specs=[pl.BlockSpec((2, None, tm, GROUP), lambda bi, m: (0, bi, m, 0)),
                  pl.BlockSpec((None, tm, GROUP), lambda bi, m: (bi, m, 10)),
                  pl.BlockSpec((1, GROUP), lambda bi, m: (0, 0)),
                  pl.BlockSpec((GROUP, GROUP), lambda bi, m: (0, 0)),
                  pl.BlockSpec((1, GROUP), lambda bi, m: (0, 0))],
        out_specs=pl.BlockSpec((None, tm, GROUP), lambda bi, m: (bi, m, 0)),
        out_shape=jax.ShapeDtypeStruct((b, t, GROUP), BF16),
        compiler_params=_cparams(2),
    )(y, z, d_skip.reshape(1, GROUP).astype(F32), glu_w.astype(BF16), glu_b.reshape(1, GROUP).astype(F32))


def _out_proj_kernel(a_ref, r_ref, n_ref, s_ref, w_ref, x_ref, mod_ref, gain_ref, o_ref, *, n_lat, tm, d):
    m = pl.program_id(1)
    mix = jnp.dot(a_ref[...], w_ref[0:GROUP, :], preferred_element_type=F32)
    for j, ref in enumerate((r_ref, n_ref, s_ref), start=1):
        mix = mix + jnp.dot(ref[...], w_ref[j * GROUP:(j + 1) * GROUP, :], preferred_element_type=F32)
    gate = _row_mod(mod_ref, 2, m, tm, n_lat, d)
    o_ref[...] = x_ref[...] + gate * (_rms_rows(mix) * gain_ref[...])


def _out_proj(parts, w_bf, x, mod, gain, n_lat):
    b, t, d = x.shape
    tm = _pick_tile(t, 528)
    part_spec = pl.BlockSpec((None, tm, GROUP), lambda bi, m: (bi, m, 0))
    row_spec = pl.BlockSpec((None, tm, d), lambda bi, m: (bi, m, 0))
    return pl.pallas_call(
        functools.partial(_out_proj_kernel, n_lat=n_lat, tm=tm, d=d),
        grid=(b, t // tm),
        in_specs=[part_spec] * 4 + [pl.BlockSpec((d, d), lambda bi, m: (0, 0)), row_spec,
                                    pl.BlockSpec((None, 2, mod.shape[-1]), lambda bi, m: (bi, 0, 0)),
                                    pl.BlockSpec((1, d), lambda bi, m: (0, 0))],
        out_specs=row_spec,
        out_shape=jax.ShapeDtypeStruct((b, t, d), F32),
        compiler_params=_cparams(2),
    )(*parts, w_bf, x, mod, gain)


def _ffn_norm_kernel(x_ref, mod_ref, gain_ref, *rest, n_lat, tm, d, routed):
    m = pl.program_id(1)
    shift = _row_mod(mod_ref, 3, m, tm, n_lat, d)
    scale = _row_mod(mod_ref, 4, m, tm, n_lat, d)
    h = (_rms_rows(x_ref[...]) * gain_ref[...]) * (1.0 + scale) + shift
    if not routed:
        (h_ref,) = rest
        h_ref[...] = h.astype(BF16)
        return
    router_ref, h_ref, route_ref = rest
    h_ref[...] = h.astype(BF16)
    logits = jnp.dot(h, router_ref[...], precision=lax.Precision.HIGHEST, preferred_element_type=F32)
    lane = lax.broadcasted_iota(jnp.int32, logits.shape, 1)
    logits = jnp.where(lane < MOE_EXPERTS, logits, -jnp.inf)
    v1 = jnp.max(logits, axis=-1, keepdims=True)
    i1 = jnp.min(jnp.where(logits == v1, lane, 128), axis=-1, keepdims=True)
    rest_l = jnp.where(lane == i1, -jnp.inf, logits)
    v2 = jnp.max(rest_l, axis=-1, keepdims=True)
    i2 = jnp.min(jnp.where(rest_l == v2, lane, 128), axis=-1, keepdims=True)
    e = jnp.exp(v2 - v1)
    w1 = 1.0 / (1.0 + e)
    w2 = e / (1.0 + e)
    route = jnp.where(lane == 0, w1, jnp.where(lane == 1, w2, jnp.where(
        lane == 2, i1.astype(F32), jnp.where(lane == 3, i2.astype(F32), 0.0))))
    route_ref[...] = route


def _ffn_norm(x, mod, gain, n_lat, router=None):
    b, t, d = x.shape
    tm = _pick_tile(t, 528)
    routed = router is not None
    row_spec = pl.BlockSpec((None, tm, d), lambda bi, m: (bi, m, 0))
    in_specs = [row_spec, pl.BlockSpec((None, 2, mod.shape[-1]), lambda bi, m: (bi, 0, 0)),
                pl.BlockSpec((1, d), lambda bi, m: (0, 0))]
    out_specs, out_shape, args = [row_spec], [jax.ShapeDtypeStruct((b, t, d), BF16)], [x, mod, gain]
    if routed:
        in_specs.append(pl.BlockSpec((d, 128), lambda bi, m: (0, 0)))
        out_specs.append(pl.BlockSpec((None, tm, 128), lambda bi, m: (bi, m, 0)))
        out_shape.append(jax.ShapeDtypeStruct((b, t, 128), F32))
        args.append(jnp.pad(router.astype(F32), ((0, 0), (0, 128 - MOE_EXPERTS))))
    return pl.pallas_call(
        functools.partial(_ffn_norm_kernel, n_lat=n_lat, tm=tm, d=d, routed=routed),
        grid=(b, t // tm), in_specs=in_specs, out_specs=out_specs, out_shape=out_shape,
        compiler_params=_cparams(2),
    )(*args)


def _swiglu_kernel(te_ref, nu_ref, x_ref, gate_ref, wg_ref, wu_ref, wd_ref, o_ref, acc_scr):
    tile = pl.program_id(0)
    f = pl.program_id(1)
    nf = pl.num_programs(1)
    used = tile < nu_ref[0]

    @pl.when(f == 0)
    def _():
        acc_scr[...] = jnp.zeros(acc_scr.shape, F32)

    @pl.when(used)
    def _():
        x = x_ref[...]
        g = jnp.dot(x, wg_ref[...], preferred_element_type=F32)
        u = jnp.dot(x, wu_ref[...], preferred_element_type=F32)
        acc_scr[...] += jnp.dot((jax.nn.silu(g) * u).astype(BF16), wd_ref[...], preferred_element_type=F32)

    @pl.when(f == nf - 1)
    def _():
        o_ref[...] = (acc_scr[...] * gate_ref[...]).astype(BF16)


def _swiglu_grouped(xs, row_gate, tile_expert, n_used, w_gate, w_up, w_down, tm):
    p, d = xs.shape
    n_exp, _, f = w_gate.shape
    tf = f // 2
    grid_spec = pltpu.PrefetchScalarGridSpec(
        num_scalar_prefetch=2,
        grid=(p // tm, f // tf),
        in_specs=[pl.BlockSpec((tm, d), lambda t, j, te, nu: (t, 0)),
                  pl.BlockSpec((tm, 1), lambda t, j, te, nu: (t, 0)),
                  pl.BlockSpec((None, d, tf), lambda t, j, te, nu: (te[t], 0, j)),
                  pl.BlockSpec((None, d, tf), lambda t, j, te, nu: (te[t], 0, j)),
                  pl.BlockSpec((None, tf, d), lambda t, j, te, nu: (te[t], j, 0))],
        out_specs=pl.BlockSpec((tm, d), lambda t, j, te, nu: (t, 0)),
        scratch_shapes=[pltpu.VMEM((tm, d), F32)])
    return pl.pallas_call(
        _swiglu_kernel, grid_spec=grid_spec,
        out_shape=jax.ShapeDtypeStruct((p, d), BF16),
        compiler_params=_cparams(2),
    )(tile_expert, n_used, xs, row_gate, w_gate, w_up, w_down)


def _ffn_residual_kernel(y_ref, x_ref, mod_ref, gain_ref, o_ref, *, n_lat, tm, d):
    m = pl.program_id(1)
    gate = _row_mod(mod_ref, 5, m, tm, n_lat, d)
    o_ref[...] = x_ref[...] + gate * (_rms_rows(y_ref[...].astype(F32)) * gain_ref[...])


def _ffn_residual(y, x, mod, gain, n_lat):
    b, t, d = x.shape
    tm = _pick_tile(t, 1056)
    row_spec = pl.BlockSpec((None, tm, d), lambda bi, m: (bi, m, 0))
    return pl.pallas_call(
        functools.partial(_ffn_residual_kernel, n_lat=n_lat, tm=tm, d=d),
        grid=(b, t // tm),
        in_specs=[row_spec, row_spec, pl.BlockSpec((None, 2, mod.shape[-1]), lambda bi, m: (bi, 0, 0)),
                  pl.BlockSpec((1, d), lambda bi, m: (0, 0))],
        out_specs=row_spec,
        out_shape=jax.ShapeDtypeStruct((b, t, d), F32),
        compiler_params=_cparams(2),
    )(y, x, mod, gain)


def _dense_ffn(h, w_gate, w_up, w_down):
    b, t, d = h.shape
    n = b * t
    tm = _pick_tile(n, 528)
    out = _swiglu_grouped(h.reshape(n, d), jnp.ones((n, 1), F32), jnp.zeros((n // tm,), jnp.int32),
                          jnp.full((1,), n // tm, jnp.int32), w_gate[None], w_up[None], w_down[None], tm)
    return out.reshape(b, t, d)


def _moe_ffn(h, route, w_gate, w_up, w_down):
    b, t, d = h.shape
    n = b * t
    tm = _pick_tile(2 * n, 512)
    route = route.reshape(n, 128)
    expert = jnp.concatenate([route[:, 2], route[:, 3]]).astype(jnp.int32)
    weight = jnp.concatenate([route[:, 0], route[:, 1]])
    token = jnp.concatenate([jnp.arange(n, dtype=jnp.int32)] * 2)
    onehot = (expert[:, None] == jnp.arange(MOE_EXPERTS, dtype=jnp.int32)[None, :]).astype(jnp.int32)
    counts = jnp.sum(onehot, axis=0)
    rank = jnp.sum((jnp.cumsum(onehot, axis=0) - onehot) * onehot, axis=1)
    padded = ((counts + tm - 1) // tm) * tm
    ends = jnp.cumsum(padded)
    pos = (ends - padded)[expert] + rank
    n_rows = 2 * n + MOE_EXPERTS * tm
    n_tiles = n_rows // tm
    tile_expert = jnp.minimum(jnp.searchsorted(ends, jnp.arange(n_tiles, dtype=jnp.int32) * tm, side='right'),
                              MOE_EXPERTS - 1).astype(jnp.int32)
    n_used = (ends[-1] // tm).astype(jnp.int32).reshape(1)
    row_token = jnp.zeros((n_rows,), jnp.int32).at[pos].set(token)
    row_gate = jnp.zeros((n_rows,), F32).at[pos].set(weight)
    xs = jnp.take(h.reshape(n, d), row_token, axis=0)
    out = _swiglu_grouped(xs, row_gate[:, None], tile_expert, n_used, w_gate, w_up, w_down, tm)
    y = jnp.take(out, pos[:n], axis=0).astype(F32) + jnp.take(out, pos[n:], axis=0).astype(F32)
    return y.reshape(b, t, d)


def kernel(x, c, ctx, c_ctx, w_mod, b_mod, norm_gain, w_in, w_out, da_lambda, da_norm_gain,
           ret_decay_logit, na_rpb, s5_a_re, s5_a_im, s5_b_re, s5_b_im, s5_c_re, s5_c_im,
           s5_log_step, s5_d, s5_glu_w, s5_glu_b, ffn_w_gate, ffn_w_up, ffn_w_down,
           moe_router, moe_w_gate, moe_w_up, moe_w_down):
    b, n_lat, d = x.shape
    n_ctx = ctx.shape[1]
    t = n_lat + n_ctx
    depth = w_mod.shape[0]
    assert d == 4 * GROUP and b + 1 <= 8

    c_rows = jnp.zeros((8, d), F32).at[:b].set(c).at[b].set(c_ctx)
    mod_all = _modulation(c_rows, w_mod, b_mod)
    tabs = _rope_tables(n_lat, t, DA_QK_DIM) + _rope_tables(n_lat, t, RET_DIM)
    h_all = jnp.concatenate([x, ctx], axis=1)

    for layer in range(depth):
        mod = jnp.stack([mod_all[layer, :b], jnp.broadcast_to(mod_all[layer, b], (b, 6 * d))], axis=1)
        g = norm_gain[layer].astype(F32)
        lam_init = 0.8 - 0.6 * math.exp(-0.3 * layer)

        z = _in_proj(h_all, mod, g[0:1], w_in[layer].astype(BF16), tabs, n_lat)
        lp = da_lambda[layer].astype(F32)
        lam = (jnp.exp(jnp.sum(lp[0] * lp[1])) - jnp.exp(jnp.sum(lp[2] * lp[3])) + lam_init).reshape(1)
        kt = jnp.swapaxes(z[:, :, GROUP:2 * GROUP], 1, 2)
        a_out = _diff_attention(z, kt, lam, da_norm_gain[layer].reshape(1, DA_V_DIM).astype(F32),
                                n_lat, 1.0 - lam_init)
        r_out = _retention(z, ret_decay_logit[layer], n_lat)
        n_out = _nbr_attention(z, na_rpb[layer], n_lat)
        ops = _s5_operators(s5_a_re[layer], s5_a_im[layer], s5_b_re[layer], s5_b_im[layer],
                            s5_c_re[layer], s5_c_im[layer], s5_log_step[layer])
        s_out = _s5_mixer(z, ops, s5_d[layer], s5_glu_w[layer], s5_glu_b[layer], n_lat)
        h_all = _out_proj((a_out, r_out, n_out, s_out), w_out[layer].astype(BF16), h_all, mod, g[1:2], n_lat)

        i = layer // 2
        if layer % 2 == 0:
            (f_in,) = _ffn_norm(h_all, mod, g[2:3], n_lat)
            f_out = _dense_ffn(f_in, ffn_w_gate[i].astype(BF16), ffn_w_up[i].astype(BF16),
                               ffn_w_down[i].astype(BF16))
        else:
            f_in, route = _ffn_norm(h_all, mod, g[2:3], n_lat, router=moe_router[i])
            f_out = _moe_ffn(f_in, route, moe_w_gate[i].astype(BF16), moe_w_up[i].astype(BF16),
                             moe_w_down[i].astype(BF16))
        h_all = _ffn_residual(f_out, h_all, mod, g[3:4], n_lat)
    return h_all[:, :n_lat]
```

```python
import functools
import math

import numpy as np
import jax
import jax.numpy as jnp
from jax import lax
from jax.experimental import pallas as pl
from jax.experimental.pallas import tpu as pltpu

F32 = jnp.float32
BF16 = jnp.bfloat16

GRID_W = 64
GROUP = 256
N_IN_SLICES = 11
DA_HEADS, DA_QK_DIM, DA_V_DIM = 4, 32, 64
RET_HEADS, RET_DIM = 4, 64
NA_HEADS, NA_DIM, NA_WIN_ROWS, NA_WIN_COLS = 4, 64, 8, 16
S5_CH, S5_GROUPS, S5_STATE = 16, 16, 64
MOE_EXPERTS, MOE_TOP_K = 8, 2
ROPE_BASE = 10000.0
NORM_EPS = 1e-6
NEG_BIG = -1e30

S5_CHUNK = 8
S5_ROW = S5_CHUNK * GROUP
S5_NSTATE = S5_GROUPS * S5_STATE
ATT_BLOCK = 256
VMEM_LIMIT = 48 * 1024 * 1024


def _cparams(n_axes):
    return pltpu.CompilerParams(dimension_semantics=("arbitrary",) * n_axes,
                                vmem_limit_bytes=VMEM_LIMIT)


def _pick_tile(n, target, mult=16):
    best = None
    for t in range(mult, min(n, target) + 1, mult):
        if n % t == 0:
            best = t
    assert best is not None, (n, target)
    return best


def _rms_rows(x):
    return x * lax.rsqrt(jnp.mean(x * x, axis=-1, keepdims=True) + NORM_EPS)


def _row_mod(mod_ref, k, m, tm, n_lat, d):
    row = m * tm + lax.broadcasted_iota(jnp.int32, (tm, 1), 0)
    return jnp.where(row >= n_lat, mod_ref[1:2, k * d:(k + 1) * d], mod_ref[0:1, k * d:(k + 1) * d])


def _mod_kernel(c_ref, w_ref, b_ref, o_ref):
    o_ref[...] = jnp.dot(jax.nn.silu(c_ref[...]), w_ref[...], precision=lax.Precision.HIGHEST,
                         preferred_element_type=F32) + b_ref[...]


def _modulation(c_rows, w_mod, b_mod):
    depth, d, n6 = w_mod.shape
    tn = 1024
    return pl.pallas_call(
        _mod_kernel,
        grid=(depth, n6 // tn),
        in_specs=[pl.BlockSpec((8, d), lambda l, n: (0, 0)),
                  pl.BlockSpec((None, d, tn), lambda l, n: (l, 0, n)),
                  pl.BlockSpec((None, 1, tn), lambda l, n: (l, 0, n))],
        out_specs=pl.BlockSpec((None, 8, tn), lambda l, n: (l, 0, n)),
        out_shape=jax.ShapeDtypeStruct((depth, 8, n6), F32),
        compiler_params=_cparams(2),
    )(c_rows, w_mod, b_mod.reshape(depth, 1, n6))


def _rope(z, cos, sin_signed, q):
    n = z.shape[-1]
    lane = lax.broadcasted_iota(jnp.int32, z.shape, 1)
    first = (lane % (2 * q)) < q
    partner = jnp.where(first, pltpu.roll(z, n - q, 1), pltpu.roll(z, q, 1))
    return z * cos + partner * sin_signed


def _in_proj_kernel(x_ref, mod_ref, gain_ref, w_ref, cda_ref, sda_ref, crt_ref, srt_ref, z_ref,
                    *, n_lat, tm, d):
    m = pl.program_id(1)
    shift = _row_mod(mod_ref, 0, m, tm, n_lat, d)
    scale = _row_mod(mod_ref, 1, m, tm, n_lat, d)
    h = ((_rms_rows(x_ref[...]) * gain_ref[...]) * (1.0 + scale) + shift).astype(BF16)
    col_scale = {0: DA_QK_DIM ** -0.5 * math.log2(math.e), 4: RET_DIM ** -0.5, 7: NA_DIM ** -0.5}
    for j in range(N_IN_SLICES):
        zj = jnp.dot(h, w_ref[:, j * GROUP:(j + 1) * GROUP], preferred_element_type=F32)
        if j in (0, 1):
            zj = _rope(zj, cda_ref[...], sda_ref[...], DA_QK_DIM // 4)
        elif j in (3, 4):
            zj = _rope(zj, crt_ref[...], srt_ref[...], RET_DIM // 4)
        if j in col_scale:
            zj = zj * col_scale[j]
        z_ref[:, j * GROUP:(j + 1) * GROUP] = zj.astype(BF16)


def _in_proj(x, mod, gain, w_bf, tabs, n_lat):
    b, t, d = x.shape
    n_out = w_bf.shape[1]
    tm = _pick_tile(t, 528)
    tab_spec = pl.BlockSpec((tm, GROUP), lambda bi, m: (m, 0))
    return pl.pallas_call(
        functools.partial(_in_proj_kernel, n_lat=n_lat, tm=tm, d=d),
        grid=(b, t // tm),
        in_specs=[pl.BlockSpec((None, tm, d), lambda bi, m: (bi, m, 0)),
                  pl.BlockSpec((None, 2, mod.shape[-1]), lambda bi, m: (bi, 0, 0)),
                  pl.BlockSpec((1, d), lambda bi, m: (0, 0)),
                  pl.BlockSpec((d, n_out), lambda bi, m: (0, 0)),
                  tab_spec, tab_spec, tab_spec, tab_spec],
        out_specs=pl.BlockSpec((None, tm, n_out), lambda bi, m: (bi, m, 0)),
        out_shape=jax.ShapeDtypeStruct((b, t, n_out), BF16),
        compiler_params=_cparams(2),
    )(x, mod, gain, w_bf, *tabs)


def _rope_tables(n_lat, t, dim):
    q = dim // 4
    tok = jnp.arange(n_lat)
    row = (tok // GRID_W).astype(F32)
    col = (tok % GRID_W).astype(F32)
    inv = ROPE_BASE ** (-jnp.arange(q, dtype=F32) / q)
    lane = np.arange(GROUP) % dim
    axis, half, qi = lane // (2 * q), (lane % (2 * q)) // q, lane % q
    pos = jnp.where(jnp.asarray(axis)[None, :] == 0, row[:, None], col[:, None])
    ang = pos * inv[jnp.asarray(qi)][None, :]
    sign = jnp.asarray(np.where(half == 0, -1.0, 1.0), F32)[None, :]
    cos = jnp.concatenate([jnp.cos(ang), jnp.ones((t - n_lat, GROUP), F32)], axis=0)
    sin = jnp.concatenate([jnp.sin(ang) * sign, jnp.zeros((t - n_lat, GROUP), F32)], axis=0)
    return cos, sin


def _diff_attn_kernel(lam_ref, qt_ref, k_ref, vta_ref, gain_ref, o_ref, qm_scr, s_scr, mx_scr, m_scr, acc_scr,
                      *, n_lat, t, tq, tk, fin_scale):
    qi = pl.program_id(1)
    n_maps = 2 * DA_HEADS
    m_scr[...] = jnp.full(m_scr.shape, -jnp.inf, F32)
    acc_scr[...] = jnp.zeros(acc_scr.shape, F32)
    qt = qt_ref[...]
    row_map = lax.broadcasted_iota(jnp.int32, (GROUP, 1), 0) // DA_QK_DIM
    for i in range(n_maps):
        qm_scr[i] = qt * (row_map == i).astype(BF16)

    def scores(off, width, i, slot):
        s = jnp.dot(k_ref[pl.ds(off, width), :], qm_scr[i], preferred_element_type=F32)
        s_scr[slot, 0:width, :] = s
        mx_scr[slot] = jnp.max(s, axis=0, keepdims=True)

    def accumulate(off, width, i, slot):
        m_prev = m_scr[i]
        m_next = jnp.maximum(m_prev, mx_scr[slot])
        p = jnp.exp2(s_scr[slot, 0:width, :] - m_next).astype(BF16)
        va = vta_ref[i // 2, :, pl.ds(off, width)]
        acc_scr[i] = jnp.exp2(m_prev - m_next) * acc_scr[i] + jnp.dot(va, p, preferred_element_type=F32)
        m_scr[i] = m_next

    is_ctx = qi * tq >= n_lat
    n_chunks = t // tk

    @pl.when(jnp.logical_not(is_ctx))
    def _():
        scores(0, tk, 0, 0)

        def body(c, carry):
            off = pl.multiple_of(c * tk, tk)
            off_next = pl.multiple_of(jnp.minimum(c + 1, n_chunks - 1) * tk, tk)
            for i in range(n_maps):
                if i + 1 < n_maps:
                    scores(off, tk, i + 1, (i + 1) % 2)
                else:
                    scores(off_next, tk, 0, 0)
                accumulate(off, tk, i, i % 2)
            return carry

        lax.fori_loop(0, n_chunks, body, 0)

    @pl.when(is_ctx)
    def _():
        for i in range(n_maps):
            scores(n_lat, t - n_lat, i, i % 2)
            accumulate(n_lat, t - n_lat, i, i % 2)

    lam = lam_ref[0]
    outs = []
    for h in range(DA_HEADS):
        a0, a1 = acc_scr[2 * h], acc_scr[2 * h + 1]
        o = (a0[0:DA_V_DIM] / a0[DA_V_DIM:DA_V_DIM + 1]
             - lam * (a1[0:DA_V_DIM] / a1[DA_V_DIM:DA_V_DIM + 1]))
        inv = lax.rsqrt(jnp.mean(o * o, axis=0, keepdims=True) + NORM_EPS)
        outs.append(o * inv * gain_ref[...] * fin_scale)
    o_ref[...] = jnp.concatenate(outs, axis=0).T.astype(BF16)


def _diff_attention(z, qt, vta, lam, gain, n_lat, fin_scale):
    b, t, _ = z.shape
    tq = ATT_BLOCK
    tk = max(w for w in (ATT_BLOCK, 2 * ATT_BLOCK, 3 * ATT_BLOCK) if t % w == 0)
    n_maps = 2 * DA_HEADS
    va_rows = vta.shape[2]
    return pl.pallas_call(
        functools.partial(_diff_attn_kernel, n_lat=n_lat, t=t, tq=tq, tk=tk, fin_scale=fin_scale),
        grid=(b, t // tq),
        in_specs=[pl.BlockSpec(memory_space=pltpu.SMEM),
                  pl.BlockSpec((None, GROUP, tq), lambda bi, m: (bi, 0, m)),
                  pl.BlockSpec((None, t, GROUP), lambda bi, m: (bi, 0, 1)),
                  pl.BlockSpec((None, DA_HEADS, va_rows, t), lambda bi, m: (bi, 0, 0, 0)),
                  pl.BlockSpec((DA_V_DIM, tq), lambda bi, m: (0, 0))],
        out_specs=pl.BlockSpec((None, tq, GROUP), lambda bi, m: (bi, m, 0)),
        out_shape=jax.ShapeDtypeStruct((b, t, GROUP), BF16),
        scratch_shapes=[pltpu.VMEM((n_maps, GROUP, tq), BF16), pltpu.VMEM((2, tk, tq), F32),
                        pltpu.VMEM((2, 1, tq), F32), pltpu.VMEM((n_maps, 1, tq), F32),
                        pltpu.VMEM((n_maps, va_rows, tq), F32)],
        compiler_params=_cparams(2),
    )(lam, qt, z, vta, gain)


def _retention_kernel(q_ref, k_ref, v_ref, g_ref, dm_ref, qw_ref, kw_ref, cd_ref, bd_ref, o_ref,
                      st_scr, of_scr, *, c, n_chunks, n_lat_chunks):
    d = pl.program_id(1)
    i = pl.program_id(2)
    chunk = jnp.where(d == 0, (i + n_lat_chunks) % n_chunks, n_chunks - 1 - i)

    @pl.when(i == 0)
    def _():
        st_scr[...] = jnp.zeros(st_scr.shape, F32)

    q, k, v = q_ref[...], k_ref[...], v_ref[...]
    lane_head = lax.broadcasted_iota(jnp.int32, (1, GROUP), 1) // RET_DIM
    o = jnp.dot((q.astype(F32) * qw_ref[...]).astype(BF16), st_scr[...].astype(BF16),
                preferred_element_type=F32)
    for h in range(RET_HEADS):
        mh = lane_head == h
        qh = q * mh.astype(BF16)
        s = lax.dot_general(qh, k, (((1,), (1,)), ((), ())), preferred_element_type=F32) * dm_ref[h]
        o = o + jnp.where(mh, jnp.dot(s.astype(BF16), v, preferred_element_type=F32), 0.0)
    kwk = (k.astype(F32) * kw_ref[...]).astype(BF16)
    kv = lax.dot_general(kwk, v, (((0,), (0,)), ((), ())), preferred_element_type=F32)
    st_scr[...] = cd_ref[...] * st_scr[...] + bd_ref[...] * kv
    off = pl.multiple_of(chunk * c, c)

    @pl.when(d == 0)
    def _():
        of_scr[pl.ds(off, c), :] = o

    @pl.when(d == 1)
    def _():
        tot = o + of_scr[pl.ds(off, c), :]
        sq = tot * tot
        inv = jnp.zeros(tot.shape, F32)
        for h in range(RET_HEADS):
            mh = lane_head == h
            ms = jnp.sum(jnp.where(mh, sq, 0.0), axis=-1, keepdims=True) * (1.0 / RET_DIM)
            inv = jnp.where(mh, lax.rsqrt(ms + NORM_EPS), inv)
        o_ref[...] = (tot * inv * jax.nn.silu(g_ref[...].astype(F32))).astype(BF16)


def _retention_tables(decay_logit, c):
    log_g = jax.nn.log_sigmoid(decay_logit.astype(F32))
    pos = jnp.arange(c, dtype=F32)
    rel = pos[:, None] - pos[None, :]
    rel = jnp.stack([rel, -rel])
    dm = jnp.where(rel[:, None] >= 0, jnp.exp(log_g[:, :, None, None] * jnp.maximum(rel[:, None], 0.0)), 0.0)
    lg_lane = jnp.repeat(log_g, RET_DIM, axis=1)
    q_exp = jnp.stack([pos + 1.0, c - pos])
    k_exp = jnp.stack([c - 1.0 - pos, pos])
    qw = jnp.exp(lg_lane[:, None, :] * q_exp[:, :, None])
    kw = jnp.exp(lg_lane[:, None, :] * k_exp[:, :, None])
    head = np.arange(GROUP) // RET_DIM
    bd = jnp.asarray((head[:, None] == head[None, :]).astype(np.float32))
    cd = jnp.exp(lg_lane * c)[:, :, None] * jnp.ones((1, 1, GROUP), F32)
    return dm, qw, kw, cd, bd


def _retention(z, decay_logit, n_lat):
    b, t, _ = z.shape
    c = ATT_BLOCK
    n_chunks, n_lat_chunks = t // c, n_lat // c
    dm, qw, kw, cd, bd = _retention_tables(decay_logit, c)

    def chunk_of(d, i):
        return jnp.where(d == 0, (i + n_lat_chunks) % n_chunks, n_chunks - 1 - i)

    def zspec(col):
        return pl.BlockSpec((None, c, GROUP), lambda bi, d, i: (bi, chunk_of(d, i), col))

    return pl.pallas_call(
        functools.partial(_retention_kernel, c=c, n_chunks=n_chunks, n_lat_chunks=n_lat_chunks),
        grid=(b, 2, n_chunks),
        in_specs=[zspec(3), zspec(4), zspec(5), zspec(6),
                  pl.BlockSpec((None, RET_HEADS, c, c), lambda bi, d, i: (d, 0, 0, 0)),
                  pl.BlockSpec((None, c, GROUP), lambda bi, d, i: (d, 0, 0)),
                  pl.BlockSpec((None, c, GROUP), lambda bi, d, i: (d, 0, 0)),
                  pl.BlockSpec((None, GROUP, GROUP), lambda bi, d, i: (d, 0, 0)),
                  pl.BlockSpec((GROUP, GROUP), lambda bi, d, i: (0, 0))],
        out_specs=pl.BlockSpec((None, c, GROUP),
                               lambda bi, d, i: (bi, jnp.where(d == 0, n_chunks - 1, n_chunks - 1 - i), 0)),
        out_shape=jax.ShapeDtypeStruct((b, t, GROUP), BF16),
        scratch_shapes=[pltpu.VMEM((GROUP, GROUP), F32), pltpu.VMEM((t, GROUP), F32)],
        compiler_params=_cparams(3),
    )(z, z, z, z, dm, qw, kw, cd, bd)


def _nbr_attn_kernel(q_ref, kp_ref, kc_ref, kn_ref, kx_ref, vp_ref, vc_ref, vn_ref, vx_ref, tab_ref, o_ref):
    q = q_ref[...]
    k = jnp.concatenate([kp_ref[...], kc_ref[...], kn_ref[...], kx_ref[...]], axis=0)
    v = jnp.concatenate([vp_ref[...], vc_ref[...], vn_ref[...], vx_ref[...]], axis=0)
    lane_head = lax.broadcasted_iota(jnp.int32, (1, GROUP), 1) // NA_DIM
    o = jnp.zeros(q.shape, F32)
    for h in range(NA_HEADS):
        mh = lane_head == h
        qh = q * mh.astype(BF16)
        s = lax.dot_general(qh, k, (((1,), (1,)), ((), ())), preferred_element_type=F32) + tab_ref[h]
        p = jnp.exp(s - jnp.max(s, axis=-1, keepdims=True))
        l = jnp.sum(p, axis=-1, keepdims=True)
        o = o + jnp.where(mh, jnp.dot(p.astype(BF16), v, preferred_element_type=F32) / l, 0.0)
    o_ref[...] = o.astype(BF16)


def _nbr_tables(rpb, n_lat, n_ctx):
    blk_rows = ATT_BLOCK // GRID_W
    rows = n_lat // GRID_W
    n_blk = rows // blk_rows
    assert rows >= NA_WIN_ROWS and n_blk >= 3
    qcol = kcol = np.arange(GRID_W)
    c0 = np.clip(qcol - NA_WIN_COLS // 2, 0, GRID_W - NA_WIN_COLS)
    col_ok = (kcol[None, :] >= c0[:, None]) & (kcol[None, :] < c0[:, None] + NA_WIN_COLS)
    dc = np.clip(kcol[None, :] - qcol[:, None] + (NA_WIN_COLS - 1), 0, 2 * NA_WIN_COLS - 2)
    oh_c = np.eye(2 * NA_WIN_COLS - 1, dtype=np.float32)[dc]
    qr, kr = np.arange(blk_rows), np.arange(3 * blk_rows)
    oh_r, oks = [], []
    for m in (0, 1, n_blk - 1):
        qrow, krow = blk_rows * m + qr, blk_rows * (m - 1) + kr
        r0 = np.clip(qrow - NA_WIN_ROWS // 2, 0, rows - NA_WIN_ROWS)
        row_ok = (krow[None, :] >= r0[:, None]) & (krow[None, :] < r0[:, None] + NA_WIN_ROWS)
        dr = np.clip(krow[None, :] - qrow[:, None] + (NA_WIN_ROWS - 1), 0, 2 * NA_WIN_ROWS - 2)
        oh_r.append(np.eye(2 * NA_WIN_ROWS - 1, dtype=np.float32)[dr])
        oks.append((row_ok[:, None, :, None] & col_ok[None, :, None, :]).reshape(ATT_BLOCK, 3 * ATT_BLOCK))
    bias = jnp.einsum('vqka,hab,QKb->vhqQkK', jnp.asarray(np.stack(oh_r)), rpb.astype(F32), jnp.asarray(oh_c),
                      precision=lax.Precision.HIGHEST).reshape(3, NA_HEADS, ATT_BLOCK, 3 * ATT_BLOCK)
    tab = jnp.where(jnp.asarray(np.stack(oks))[:, None], bias, NEG_BIG)
    tab = jnp.concatenate([tab, jnp.full((1, NA_HEADS, ATT_BLOCK, 3 * ATT_BLOCK), NEG_BIG, F32)], axis=0)
    return jnp.concatenate([tab, jnp.zeros((4, NA_HEADS, ATT_BLOCK, n_ctx), F32)], axis=-1)


def _nbr_attention(z, rpb, n_lat):
    b, t, _ = z.shape
    n_ctx = t - n_lat
    assert n_ctx == ATT_BLOCK and n_lat % ATT_BLOCK == 0
    n_blk = n_lat // ATT_BLOCK
    tab = _nbr_tables(rpb, n_lat, n_ctx)

    def kv_specs(col):
        mk = lambda f: pl.BlockSpec((None, ATT_BLOCK, GROUP), lambda bi, m: (bi, f(m), col))
        return [mk(lambda m: jnp.clip(m - 1, 0, n_blk - 1)), mk(lambda m: jnp.minimum(m, n_blk - 1)),
                mk(lambda m: jnp.clip(m + 1, 0, n_blk - 1)), mk(lambda m: n_blk)]

    def variant(m):
        return jnp.where(m == 0, 0, jnp.where(m == n_blk - 1, 2, jnp.where(m == n_blk, 3, 1)))

    win = 3 * ATT_BLOCK + n_ctx
    return pl.pallas_call(
        _nbr_attn_kernel,
        grid=(b, n_blk + 1),
        in_specs=[pl.BlockSpec((None, ATT_BLOCK, GROUP), lambda bi, m: (bi, m, 7))] + kv_specs(8) + kv_specs(9)
                 + [pl.BlockSpec((None, NA_HEADS, ATT_BLOCK, win), lambda bi, m: (variant(m), 0, 0, 0))],
        out_specs=pl.BlockSpec((None, ATT_BLOCK, GROUP), lambda bi, m: (bi, m, 0)),
        out_shape=jax.ShapeDtypeStruct((b, t, GROUP), BF16),
        compiler_params=_cparams(2),
    )(*([z] * 9), tab)


def _s5_kernel(u_ref, x0_ref, mm_ref, sm_ref, rm_ref, ap_ref, y_ref, xf_ref, st_scr, sloc_scr, xin_scr,
               *, rb):
    d = pl.program_id(0)
    i = pl.program_id(2)

    @pl.when(i == 0)
    def _():
        st_scr[...] = x0_ref[...]

    u = u_ref[...]
    sloc_scr[...] = jnp.dot(u, sm_ref[...], preferred_element_type=F32)
    ar, ai = ap_ref[0:1, :], ap_ref[1:2, :]
    ns = S5_NSTATE

    def step(n, carry):
        xr, xi = carry
        r = jnp.where(d == 0, n, rb - 1 - n)
        xin_scr[pl.ds(r, 1), 0:ns] = xr
        xin_scr[pl.ds(r, 1), ns:2 * ns] = xi
        s = sloc_scr[pl.ds(r, 1), :]
        return ar * xr - ai * xi + s[:, 0:ns], ar * xi + ai * xr + s[:, ns:2 * ns]

    xr, xi = lax.fori_loop(0, rb, step, (st_scr[:, 0:ns], st_scr[:, ns:2 * ns]))
    st_scr[:, 0:ns] = xr
    st_scr[:, ns:2 * ns] = xi
    y_ref[...] = (jnp.dot(u, mm_ref[...], preferred_element_type=F32)
                  + jnp.dot(xin_scr[...].astype(BF16), rm_ref[...], preferred_element_type=F32))
    xf_ref[...] = st_scr[...]


def _s5_operators(a_re, a_im, b_re, b_im, c_re, c_im, log_step):
    tc, g, p, ch = S5_CHUNK, S5_GROUPS, S5_STATE, S5_CH
    hi = lax.Precision.HIGHEST
    ar, ai = a_re.astype(F32), a_im.astype(F32)
    dt = jnp.exp(log_step.astype(F32))[..., None]
    steps = jnp.arange(tc + 1, dtype=F32)[None, :, None, None]
    mag = jnp.exp((ar * dt)[:, None] * steps)
    ang = (ai * dt)[:, None] * steps
    pw_r, pw_i = mag * jnp.cos(ang), mag * jnp.sin(ang)
    x, y, den = pw_r[:, 1] - 1.0, pw_i[:, 1], ar * ar + ai * ai
    cf_r, cf_i = ((x * ar + y * ai) / den)[..., None], ((y * ar - x * ai) / den)[..., None]
    bre, bim = b_re.astype(F32), b_im.astype(F32)
    bb_r, bb_i = cf_r * bre - cf_i * bim, cf_r * bim + cf_i * bre
    cr, ci = c_re.astype(F32), c_im.astype(F32)
    w_r = pw_r[..., None] * bb_r[:, None] - pw_i[..., None] * bb_i[:, None]
    w_i = pw_r[..., None] * bb_i[:, None] + pw_i[..., None] * bb_r[:, None]
    kern = (jnp.einsum('dgcp,dtgph->dtghc', cr, w_r, precision=hi)
            - jnp.einsum('dgcp,dtgph->dtghc', ci, w_i, precision=hi))
    eye = jnp.eye(g, dtype=F32)
    idx = np.arange(tc)
    outs = []
    for d in range(2):
        lag = (idx[None, :] - idx[:, None]) if d == 0 else (idx[:, None] - idx[None, :])
        kt = jnp.where(jnp.asarray(lag >= 0)[:, :, None, None, None],
                       kern[d][jnp.asarray(np.maximum(lag, 0))], 0.0)
        mm = (jnp.transpose(kt, (0, 2, 3, 1, 4))[:, :, :, :, None, :]
              * eye[None, :, None, None, :, None]).reshape(tc * g * ch, tc * g * ch)
        f_exp = jnp.asarray((tc - 1 - idx) if d == 0 else idx)
        sm = jnp.concatenate(
            [(jnp.transpose(part[d][f_exp], (0, 1, 3, 2))[:, :, :, None, :]
              * eye[None, :, None, :, None]).reshape(tc * g * ch, g * p) for part in (w_r, w_i)], axis=1)
        e_exp = jnp.asarray((idx + 1) if d == 0 else (tc - idx))
        pr, pi_ = pw_r[d][e_exp][:, :, None, :], pw_i[d][e_exp][:, :, None, :]
        ca_r, ca_i = cr[d][None] * pr - ci[d][None] * pi_, cr[d][None] * pi_ + ci[d][None] * pr
        rm = jnp.concatenate(
            [(jnp.transpose(part, (1, 3, 0, 2))[:, :, :, None, :]
              * eye[:, None, None, :, None]).reshape(g * p, tc * g * ch) for part in (ca_r, -ca_i)], axis=0)
        ap = jnp.stack([pw_r[d, tc].reshape(-1), pw_i[d, tc].reshape(-1)])
        outs.append((mm, sm, rm, ap))
    mm, sm, rm, ap = (jnp.stack(v) for v in zip(*outs))
    return mm.astype(BF16), sm.astype(BF16), rm.astype(BF16), ap


def _s5_scan(u_rows, x0, ops, rb):
    b, r, _ = u_rows.shape
    nb = r // rb
    mm, sm, rm, ap = ops
    blk = lambda d, i: jnp.where(d == 0, i, nb - 1 - i)
    op_spec = lambda shp: pl.BlockSpec((None,) + shp, lambda d, bi, i: (d, 0, 0), pipeline_mode=pl.Buffered(1))
    return pl.pallas_call(
        functools.partial(_s5_kernel, rb=rb),
        grid=(2, b, nb),
        in_specs=[pl.BlockSpec((None, rb, S5_ROW), lambda d, bi, i: (bi, blk(d, i), 0)),
                  pl.BlockSpec((None, None, 1, 2 * S5_NSTATE), lambda d, bi, i: (d, bi, 0, 0)),
                  op_spec((S5_ROW, S5_ROW)), op_spec((S5_ROW, 2 * S5_NSTATE)),
                  op_spec((2 * S5_NSTATE, S5_ROW)),
                  pl.BlockSpec((None, 2, S5_NSTATE), lambda d, bi, i: (d, 0, 0))],
        out_specs=[pl.BlockSpec((None, None, rb, S5_ROW), lambda d, bi, i: (d, bi, blk(d, i), 0)),
                   pl.BlockSpec((None, None, 1, 2 * S5_NSTATE), lambda d, bi, i: (d, bi, 0, 0))],
        out_shape=[jax.ShapeDtypeStruct((2, b, r, S5_ROW), F32),
                   jax.ShapeDtypeStruct((2, b, 1, 2 * S5_NSTATE), F32)],
        scratch_shapes=[pltpu.VMEM((1, 2 * S5_NSTATE), F32), pltpu.VMEM((rb, 2 * S5_NSTATE), F32),
                        pltpu.VMEM((rb, 2 * S5_NSTATE), F32)],
        compiler_params=_cparams(3),
    )(u_rows, x0, mm, sm, rm, ap)


def _s5_glu_kernel(y_ref, u_ref, d_ref, w_ref, b_ref, o_ref):
    y = u_ref[...].astype(F32) * d_ref[...] + y_ref[0] + y_ref[1]
    act = jax.nn.gelu(y)
    gate = jnp.dot(act.astype(BF16), w_ref[...], preferred_element_type=F32) + b_ref[...]
    o_ref[...] = (act * jax.nn.sigmoid(gate)).astype(BF16)


def _s5_mixer(z, ops, d_skip, glu_w, glu_b, n_lat):
    b, t, _ = z.shape
    u = z[:, :, 10 * GROUP:11 * GROUP]
    u_lat = u[:, :n_lat].reshape(b, n_lat // S5_CHUNK, S5_ROW)
    u_ctx = u[:, n_lat:].reshape(b, (t - n_lat) // S5_CHUNK, S5_ROW)
    zero = jnp.zeros((2, b, 1, 2 * S5_NSTATE), F32)
    y_ctx, x_ctx = _s5_scan(u_ctx, zero, ops, u_ctx.shape[1])
    y_lat, _ = _s5_scan(u_lat, x_ctx, ops, _pick_tile(u_lat.shape[1], 256))
    y = jnp.concatenate([y_lat, y_ctx], axis=2).reshape(2, b, t, GROUP)
    tm = _pick_tile(t, 1056)
    return pl.pallas_call(
        _s5_glu_kernel,
        grid=(b, t // tm),
        in_specs=[pl.BlockSpec((2, None, tm, GROUP), lambda bi, m: (0, bi, m, 0)),
                  pl.BlockSpec((None, tm, GROUP), lambda bi, m: (bi, m, 10)),
                  pl.BlockSpec((1, GROUP), lambda bi, m: (0, 0)),
                  pl.BlockSpec((GROUP, GROUP), lambda bi, m: (0, 0)),
                  pl.BlockSpec((1, GROUP), lambda bi, m: (0, 0))],
        out_specs=pl.BlockSpec((None, tm, GROUP), lambda bi, m: (bi, m, 0)),
        out_shape=jax.ShapeDtypeStruct((b, t, GROUP), BF16),
        compiler_params=_cparams(2),
    )(y, z, d_skip.reshape(1, GROUP).astype(F32), glu_w.astype(BF16), glu_b.reshape(1, GROUP).astype(F32))


def _out_proj_kernel(a_ref, r_ref, n_ref, s_ref, w_ref, x_ref, mod_ref, gain_ref, o_ref, *, n_lat, tm, d):
    m = pl.program_id(1)
    mix = jnp.dot(a_ref[...], w_ref[0:GROUP, :], preferred_element_type=F32)
    for j, ref in enumerate((r_ref, n_ref, s_ref), start=1):
        mix = mix + jnp.dot(ref[...], w_ref[j * GROUP:(j + 1) * GROUP, :], preferred_element_type=F32)
    gate = _row_mod(mod_ref, 2, m, tm, n_lat, d)
    o_ref[...] = x_ref[...] + gate * (_rms_rows(mix) * gain_ref[...])


def _out_proj(parts, w_bf, x, mod, gain, n_lat):
    b, t, d = x.shape
    tm = _pick_tile(t, 528)
    part_spec = pl.BlockSpec((None, tm, GROUP), lambda bi, m: (bi, m, 0))
    row_spec = pl.BlockSpec((None, tm, d), lambda bi, m: (bi, m, 0))
    return pl.pallas_call(
        functools.partial(_out_proj_kernel, n_lat=n_lat, tm=tm, d=d),
        grid=(b, t // tm),
        in_specs=[part_spec] * 4 + [pl.BlockSpec((d, d), lambda bi, m: (0, 0)), row_spec,
                                    pl.BlockSpec((None, 2, mod.shape[-1]), lambda bi, m: (bi, 0, 0)),
                                    pl.BlockSpec((1, d), lambda bi, m: (0, 0))],
        out_specs=row_spec,
        out_shape=jax.ShapeDtypeStruct((b, t, d), F32),
        compiler_params=_cparams(2),
    )(*parts, w_bf, x, mod, gain)


def _ffn_norm_kernel(x_ref, mod_ref, gain_ref, *rest, n_lat, tm, d, routed):
    m = pl.program_id(1)
    shift = _row_mod(mod_ref, 3, m, tm, n_lat, d)
    scale = _row_mod(mod_ref, 4, m, tm, n_lat, d)
    h = (_rms_rows(x_ref[...]) * gain_ref[...]) * (1.0 + scale) + shift
    if not routed:
        (h_ref,) = rest
        h_ref[...] = h.astype(BF16)
        return
    router_ref, h_ref, route_ref = rest
    h_ref[...] = h.astype(BF16)
    logits = jnp.dot(h, router_ref[...], precision=lax.Precision.HIGHEST, preferred_element_type=F32)
    lane = lax.broadcasted_iota(jnp.int32, logits.shape, 1)
    logits = jnp.where(lane < MOE_EXPERTS, logits, -jnp.inf)
    v1 = jnp.max(logits, axis=-1, keepdims=True)
    i1 = jnp.min(jnp.where(logits == v1, lane, 128), axis=-1, keepdims=True)
    rest_l = jnp.where(lane == i1, -jnp.inf, logits)
    v2 = jnp.max(rest_l, axis=-1, keepdims=True)
    i2 = jnp.min(jnp.where(rest_l == v2, lane, 128), axis=-1, keepdims=True)
    e = jnp.exp(v2 - v1)
    w1 = 1.0 / (1.0 + e)
    w2 = e / (1.0 + e)
    route = jnp.where(lane == 0, w1, jnp.where(lane == 1, w2, jnp.where(
        lane == 2, i1.astype(F32), jnp.where(lane == 3, i2.astype(F32), 0.0))))
    route_ref[...] = route


def _ffn_norm(x, mod, gain, n_lat, router=None):
    b, t, d = x.shape
    tm = _pick_tile(t, 528)
    routed = router is not None
    row_spec = pl.BlockSpec((None, tm, d), lambda bi, m: (bi, m, 0))
    in_specs = [row_spec, pl.BlockSpec((None, 2, mod.shape[-1]), lambda bi, m: (bi, 0, 0)),
                pl.BlockSpec((1, d), lambda bi, m: (0, 0))]
    out_specs, out_shape, args = [row_spec], [jax.ShapeDtypeStruct((b, t, d), BF16)], [x, mod, gain]
    if routed:
        in_specs.append(pl.BlockSpec((d, 128), lambda bi, m: (0, 0)))
        out_specs.append(pl.BlockSpec((None, tm, 128), lambda bi, m: (bi, m, 0)))
        out_shape.append(jax.ShapeDtypeStruct((b, t, 128), F32))
        args.append(jnp.pad(router.astype(F32), ((0, 0), (0, 128 - MOE_EXPERTS))))
    return pl.pallas_call(
        functools.partial(_ffn_norm_kernel, n_lat=n_lat, tm=tm, d=d, routed=routed),
        grid=(b, t // tm), in_specs=in_specs, out_specs=out_specs, out_shape=out_shape,
        compiler_params=_cparams(2),
    )(*args)


def _swiglu_kernel(te_ref, nu_ref, x_ref, gate_ref, wg_ref, wu_ref, wd_ref, o_ref, acc_scr):
    tile = pl.program_id(0)
    f = pl.program_id(1)
    nf = pl.num_programs(1)
    used = tile < nu_ref[0]

    @pl.when(f == 0)
    def _():
        acc_scr[...] = jnp.zeros(acc_scr.shape, F32)

    @pl.when(used)
    def _():
        x = x_ref[...]
        g = jnp.dot(x, wg_ref[...], preferred_element_type=F32)
        u = jnp.dot(x, wu_ref[...], preferred_element_type=F32)
        acc_scr[...] += jnp.dot((jax.nn.silu(g) * u).astype(BF16), wd_ref[...], preferred_element_type=F32)

    @pl.when(f == nf - 1)
    def _():
        o_ref[...] = (acc_scr[...] * gate_ref[...]).astype(BF16)


def _swiglu_grouped(xs, row_gate, tile_expert, n_used, w_gate, w_up, w_down, tm):
    p, d = xs.shape
    n_exp, _, f = w_gate.shape
    tf = f // 2
    grid_spec = pltpu.PrefetchScalarGridSpec(
        num_scalar_prefetch=2,
        grid=(p // tm, f // tf),
        in_specs=[pl.BlockSpec((tm, d), lambda t, j, te, nu: (t, 0)),
                  pl.BlockSpec((tm, 1), lambda t, j, te, nu: (t, 0)),
                  pl.BlockSpec((None, d, tf), lambda t, j, te, nu: (te[t], 0, j)),
                  pl.BlockSpec((None, d, tf), lambda t, j, te, nu: (te[t], 0, j)),
                  pl.BlockSpec((None, tf, d), lambda t, j, te, nu: (te[t], j, 0))],
        out_specs=pl.BlockSpec((tm, d), lambda t, j, te, nu: (t, 0)),
        scratch_shapes=[pltpu.VMEM((tm, d), F32)])
    return pl.pallas_call(
        _swiglu_kernel, grid_spec=grid_spec,
        out_shape=jax.ShapeDtypeStruct((p, d), BF16),
        compiler_params=_cparams(2),
    )(tile_expert, n_used, xs, row_gate, w_gate, w_up, w_down)


def _ffn_residual_kernel(y_ref, x_ref, mod_ref, gain_ref, o_ref, *, n_lat, tm, d):
    m = pl.program_id(1)
    gate = _row_mod(mod_ref, 5, m, tm, n_lat, d)
    o_ref[...] = x_ref[...] + gate * (_rms_rows(y_ref[...].astype(F32)) * gain_ref[...])


def _ffn_residual(y, x, mod, gain, n_lat):
    b, t, d = x.shape
    tm = _pick_tile(t, 1056)
    row_spec = pl.BlockSpec((None, tm, d), lambda bi, m: (bi, m, 0))
    return pl.pallas_call(
        functools.partial(_ffn_residual_kernel, n_lat=n_lat, tm=tm, d=d),
        grid=(b, t // tm),
        in_specs=[row_spec, row_spec, pl.BlockSpec((None, 2, mod.shape[-1]), lambda bi, m: (bi, 0, 0)),
                  pl.BlockSpec((1, d), lambda bi, m: (0, 0))],
        out_specs=row_spec,
        out_shape=jax.ShapeDtypeStruct((b, t, d), F32),
        compiler_params=_cparams(2),
    )(y, x, mod, gain)


def _dense_ffn(h, w_gate, w_up, w_down):
    b, t, d = h.shape
    n = b * t
    tm = _pick_tile(n, 528)
    out = _swiglu_grouped(h.reshape(n, d), jnp.ones((n, 1), F32), jnp.zeros((n // tm,), jnp.int32),
                          jnp.full((1,), n // tm, jnp.int32), w_gate[None], w_up[None], w_down[None], tm)
    return out.reshape(b, t, d)


def _moe_ffn(h, route, w_gate, w_up, w_down):
    b, t, d = h.shape
    n = b * t
    tm = _pick_tile(2 * n, 512)
    route = route.reshape(n, 128)
    expert = jnp.concatenate([route[:, 2], route[:, 3]]).astype(jnp.int32)
    weight = jnp.concatenate([route[:, 0], route[:, 1]])
    token = jnp.concatenate([jnp.arange(n, dtype=jnp.int32)] * 2)
    onehot = (expert[:, None] == jnp.arange(MOE_EXPERTS, dtype=jnp.int32)[None, :]).astype(jnp.int32)
    counts = jnp.sum(onehot, axis=0)
    rank = jnp.sum((jnp.cumsum(onehot, axis=0) - onehot) * onehot, axis=1)
    padded = ((counts + tm - 1) // tm) * tm
    ends = jnp.cumsum(padded)
    pos = (ends - padded)[expert] + rank
    n_rows = 2 * n + MOE_EXPERTS * tm
    n_tiles = n_rows // tm
    tile_expert = jnp.minimum(jnp.searchsorted(ends, jnp.arange(n_tiles, dtype=jnp.int32) * tm, side='right'),
                              MOE_EXPERTS - 1).astype(jnp.int32)
    n_used = (ends[-1] // tm).astype(jnp.int32).reshape(1)
    row_token = jnp.zeros((n_rows,), jnp.int32).at[pos].set(token)
    row_gate = jnp.zeros((n_rows,), F32).at[pos].set(weight)
    xs = jnp.take(h.reshape(n, d), row_token, axis=0)
    out = _swiglu_grouped(xs, row_gate[:, None], tile_expert, n_used, w_gate, w_up, w_down, tm)
    y = jnp.take(out, pos[:n], axis=0).astype(F32) + jnp.take(out, pos[n:], axis=0).astype(F32)
    return y.reshape(b, t, d)


def kernel(x, c, ctx, c_ctx, w_mod, b_mod, norm_gain, w_in, w_out, da_lambda, da_norm_gain,
           ret_decay_logit, na_rpb, s5_a_re, s5_a_im, s5_b_re, s5_b_im, s5_c_re, s5_c_im,
           s5_log_step, s5_d, s5_glu_w, s5_glu_b, ffn_w_gate, ffn_w_up, ffn_w_down,
           moe_router, moe_w_gate, moe_w_up, moe_w_down):
    b, n_lat, d = x.shape
    n_ctx = ctx.shape[1]
    t = n_lat + n_ctx
    depth = w_mod.shape[0]
    assert d == 4 * GROUP and b + 1 <= 8

    c_rows = jnp.zeros((8, d), F32).at[:b].set(c).at[b].set(c_ctx)
    mod_all = _modulation(c_rows, w_mod, b_mod)
    tabs = _rope_tables(n_lat, t, DA_QK_DIM) + _rope_tables(n_lat, t, RET_DIM)
    h_all = jnp.concatenate([x, ctx], axis=1)

    for layer in range(depth):
        mod = jnp.stack([mod_all[layer, :b], jnp.broadcast_to(mod_all[layer, b], (b, 6 * d))], axis=1)
        g = norm_gain[layer].astype(F32)
        lam_init = 0.8 - 0.6 * math.exp(-0.3 * layer)

        z = _in_proj(h_all, mod, g[0:1], w_in[layer].astype(BF16), tabs, n_lat)
        lp = da_lambda[layer].astype(F32)
        lam = (jnp.exp(jnp.sum(lp[0] * lp[1])) - jnp.exp(jnp.sum(lp[2] * lp[3])) + lam_init).reshape(1)
        qt = jnp.swapaxes(z[:, :, 0:GROUP], 1, 2)
        vt = jnp.swapaxes(z[:, :, 2 * GROUP:3 * GROUP], 1, 2).reshape(b, DA_HEADS, DA_V_DIM, t)
        vta = jnp.concatenate([vt, jnp.ones((b, DA_HEADS, 16, t), BF16)], axis=2)
        da_gain = jnp.broadcast_to(da_norm_gain[layer].astype(F32)[:, None], (DA_V_DIM, ATT_BLOCK))
        a_out = _diff_attention(z, qt, vta, lam, da_gain, n_lat, 1.0 - lam_init)
        r_out = _retention(z, ret_decay_logit[layer], n_lat)
        n_out = _nbr_attention(z, na_rpb[layer], n_lat)
        ops = _s5_operators(s5_a_re[layer], s5_a_im[layer], s5_b_re[layer], s5_b_im[layer],
                            s5_c_re[layer], s5_c_im[layer], s5_log_step[layer])
        s_out = _s5_mixer(z, ops, s5_d[layer], s5_glu_w[layer], s5_glu_b[layer], n_lat)
        h_all = _out_proj((a_out, r_out, n_out, s_out), w_out[layer].astype(BF16), h_all, mod, g[1:2], n_lat)

        i = layer // 2
        if layer % 2 == 0:
            (f_in,) = _ffn_norm(h_all, mod, g[2:3], n_lat)
            f_out = _dense_ffn(f_in, ffn_w_gate[i].astype(BF16), ffn_w_up[i].astype(BF16),
                               ffn_w_down[i].astype(BF16))
        else:
            f_in, route = _ffn_norm(h_all, mod, g[2:3], n_lat, router=moe_router[i])
            f_out = _moe_ffn(f_in, route, moe_w_gate[i].astype(BF16), moe_w_up[i].astype(BF16),
                             moe_w_down[i].astype(BF16))
        h_all = _ffn_residual(f_out, h_all, mod, g[3:4], n_lat)
    return h_all[:, :n_lat]
```

```python
import functools
import math

import numpy as np
import jax
import jax.numpy as jnp
from jax import lax
from jax.experimental import pallas as pl
from jax.experimental.pallas import tpu as pltpu

F32 = jnp.float32
BF16 = jnp.bfloat16

GRID_W = 64
GROUP = 256
N_IN_SLICES = 11
DA_HEADS, DA_QK_DIM, DA_V_DIM = 4, 32, 64
DA_ONES_ROWS = 16
RET_HEADS, RET_DIM = 4, 64
NA_HEADS, NA_DIM, NA_WIN_ROWS, NA_WIN_COLS = 4, 64, 8, 16
S5_CH, S5_GROUPS, S5_STATE = 16, 16, 64
MOE_EXPERTS, MOE_TOP_K = 8, 2
ROPE_BASE = 10000.0
NORM_EPS = 1e-6
NEG_BIG = -1e30

S5_CHUNK = 8
S5_ROW = S5_CHUNK * GROUP
S5_NSTATE = S5_GROUPS * S5_STATE
ATT_BLOCK = 256
VMEM_LIMIT = 48 * 1024 * 1024


def _cparams(n_axes):
    return pltpu.CompilerParams(dimension_semantics=("arbitrary",) * n_axes,
                                vmem_limit_bytes=VMEM_LIMIT)


def _pick_tile(n, target, mult=16):
    best = None
    for t in range(mult, min(n, target) + 1, mult):
        if n % t == 0:
            best = t
    assert best is not None, (n, target)
    return best


def _rms_rows(x):
    return x * lax.rsqrt(jnp.mean(x * x, axis=-1, keepdims=True) + NORM_EPS)


def _row_mod(mod_ref, k, m, tm, n_lat, d):
    row = m * tm + lax.broadcasted_iota(jnp.int32, (tm, 1), 0)
    return jnp.where(row >= n_lat, mod_ref[1:2, k * d:(k + 1) * d], mod_ref[0:1, k * d:(k + 1) * d])


def _mod_kernel(c_ref, w_ref, b_ref, o_ref):
    o_ref[...] = jnp.dot(jax.nn.silu(c_ref[...]), w_ref[...], precision=lax.Precision.HIGHEST,
                         preferred_element_type=F32) + b_ref[...]


def _modulation(c_rows, w_mod, b_mod):
    depth, d, n6 = w_mod.shape
    tn = 1024
    return pl.pallas_call(
        _mod_kernel,
        grid=(depth, n6 // tn),
        in_specs=[pl.BlockSpec((8, d), lambda l, n: (0, 0)),
                  pl.BlockSpec((None, d, tn), lambda l, n: (l, 0, n)),
                  pl.BlockSpec((None, 1, tn), lambda l, n: (l, 0, n))],
        out_specs=pl.BlockSpec((None, 8, tn), lambda l, n: (l, 0, n)),
        out_shape=jax.ShapeDtypeStruct((depth, 8, n6), F32),
        compiler_params=_cparams(2),
    )(c_rows, w_mod, b_mod.reshape(depth, 1, n6))


def _rope(z, cos, sin_signed, q):
    n = z.shape[-1]
    lane = lax.broadcasted_iota(jnp.int32, z.shape, 1)
    first = (lane % (2 * q)) < q
    partner = jnp.where(first, pltpu.roll(z, n - q, 1), pltpu.roll(z, q, 1))
    return z * cos + partner * sin_signed


def _in_proj_kernel(x_ref, mod_ref, gain_ref, w_ref, cda_ref, sda_ref, crt_ref, srt_ref,
                    z_ref, qt_ref, vta_ref, uf_ref, u_scr, *, n_lat, tm, d):
    m = pl.program_id(1)
    shift = _row_mod(mod_ref, 0, m, tm, n_lat, d)
    scale = _row_mod(mod_ref, 1, m, tm, n_lat, d)
    h = ((_rms_rows(x_ref[...]) * gain_ref[...]) * (1.0 + scale) + shift).astype(BF16)
    col_scale = {0: DA_QK_DIM ** -0.5 * math.log2(math.e), 4: RET_DIM ** -0.5, 7: NA_DIM ** -0.5}
    for j in range(N_IN_SLICES):
        zj = jnp.dot(h, w_ref[:, j * GROUP:(j + 1) * GROUP], preferred_element_type=F32)
        if j in (0, 1):
            zj = _rope(zj, cda_ref[...], sda_ref[...], DA_QK_DIM // 4)
        elif j in (3, 4):
            zj = _rope(zj, crt_ref[...], srt_ref[...], RET_DIM // 4)
        if j in col_scale:
            zj = zj * col_scale[j]
        z_ref[:, j * GROUP:(j + 1) * GROUP] = zj.astype(BF16)
        if j == 0:
            qt_ref[...] = zj.T.astype(BF16)
        elif j == 2:
            vta_ref[:, 0:DA_V_DIM, :] = zj.T.reshape(DA_HEADS, DA_V_DIM, tm).astype(BF16)
            vta_ref[:, DA_V_DIM:, :] = jnp.ones((DA_HEADS, DA_ONES_ROWS, tm), BF16)
        elif j == 10:
            for hf in range(GROUP // 128):
                u_scr[hf] = zj[:, hf * 128:(hf + 1) * 128]
            for r in range(S5_CHUNK):
                for hf in range(GROUP // 128):
                    lo = r * GROUP + hf * 128
                    uf_ref[:, lo:lo + 128] = u_scr[hf, pl.ds(r, tm // S5_CHUNK, stride=S5_CHUNK), :].astype(BF16)


def _in_proj(x, mod, gain, w_bf, tabs, n_lat):
    b, t, d = x.shape
    n_out = w_bf.shape[1]
    tm = _pick_tile(t, 768, mult=128)
    tab_spec = pl.BlockSpec((tm, GROUP), lambda bi, m: (m, 0))
    va_rows = DA_V_DIM + DA_ONES_ROWS
    return pl.pallas_call(
        functools.partial(_in_proj_kernel, n_lat=n_lat, tm=tm, d=d),
        grid=(b, t // tm),
        in_specs=[pl.BlockSpec((None, tm, d), lambda bi, m: (bi, m, 0)),
                  pl.BlockSpec((None, 2, mod.shape[-1]), lambda bi, m: (bi, 0, 0)),
                  pl.BlockSpec((1, d), lambda bi, m: (0, 0)),
                  pl.BlockSpec((d, n_out), lambda bi, m: (0, 0)),
                  tab_spec, tab_spec, tab_spec, tab_spec],
        out_specs=[pl.BlockSpec((None, tm, n_out), lambda bi, m: (bi, m, 0)),
                   pl.BlockSpec((None, GROUP, tm), lambda bi, m: (bi, 0, m)),
                   pl.BlockSpec((None, DA_HEADS, va_rows, tm), lambda bi, m: (bi, 0, 0, m)),
                   pl.BlockSpec((None, tm // S5_CHUNK, S5_ROW), lambda bi, m: (bi, m, 0))],
        out_shape=[jax.ShapeDtypeStruct((b, t, n_out), BF16),
                   jax.ShapeDtypeStruct((b, GROUP, t), BF16),
                   jax.ShapeDtypeStruct((b, DA_HEADS, va_rows, t), BF16),
                   jax.ShapeDtypeStruct((b, t // S5_CHUNK, S5_ROW), BF16)],
        scratch_shapes=[pltpu.VMEM((GROUP // 128, tm, 128), F32)],
        compiler_params=_cparams(2),
    )(x, mod, gain, w_bf, *tabs)


def _rope_tables(n_lat, t, dim):
    q = dim // 4
    tok = jnp.arange(n_lat)
    row = (tok // GRID_W).astype(F32)
    col = (tok % GRID_W).astype(F32)
    inv = ROPE_BASE ** (-jnp.arange(q, dtype=F32) / q)
    lane = np.arange(GROUP) % dim
    axis, half, qi = lane // (2 * q), (lane % (2 * q)) // q, lane % q
    pos = jnp.where(jnp.asarray(axis)[None, :] == 0, row[:, None], col[:, None])
    ang = pos * inv[jnp.asarray(qi)][None, :]
    sign = jnp.asarray(np.where(half == 0, -1.0, 1.0), F32)[None, :]
    cos = jnp.concatenate([jnp.cos(ang), jnp.ones((t - n_lat, GROUP), F32)], axis=0)
    sin = jnp.concatenate([jnp.sin(ang) * sign, jnp.zeros((t - n_lat, GROUP), F32)], axis=0)
    return cos, sin


def _diff_attn_kernel(lam_ref, qt_ref, k_ref, vta_ref, gain_ref, o_ref, qm_scr, s_scr, mx_scr, m_scr, acc_scr,
                      *, n_lat, t, tq, tk, fin_scale):
    qi = pl.program_id(1)
    n_maps = 2 * DA_HEADS
    m_scr[...] = jnp.full(m_scr.shape, -jnp.inf, F32)
    acc_scr[...] = jnp.zeros(acc_scr.shape, F32)
    qt = qt_ref[...]
    row_map = lax.broadcasted_iota(jnp.int32, (GROUP, 1), 0) // DA_QK_DIM
    for i in range(n_maps):
        qm_scr[i] = qt * (row_map == i).astype(BF16)

    def scores(off, width, i, slot):
        s = jnp.dot(k_ref[pl.ds(off, width), :], qm_scr[i], preferred_element_type=F32)
        s_scr[slot, 0:width, :] = s
        mx_scr[slot] = jnp.max(s, axis=0, keepdims=True)

    def accumulate(off, width, i, slot):
        m_prev = m_scr[i]
        m_next = jnp.maximum(m_prev, mx_scr[slot])
        p = jnp.exp2(s_scr[slot, 0:width, :] - m_next).astype(BF16)
        va = vta_ref[i // 2, :, pl.ds(off, width)]
        acc_scr[i] = jnp.exp2(m_prev - m_next) * acc_scr[i] + jnp.dot(va, p, preferred_element_type=F32)
        m_scr[i] = m_next

    is_ctx = qi * tq >= n_lat
    n_chunks = t // tk

    @pl.when(jnp.logical_not(is_ctx))
    def _():
        scores(0, tk, 0, 0)

        def body(c, carry):
            off = pl.multiple_of(c * tk, tk)
            off_next = pl.multiple_of(jnp.minimum(c + 1, n_chunks - 1) * tk, tk)
            for i in range(n_maps):
                if i + 1 < n_maps:
                    scores(off, tk, i + 1, (i + 1) % 2)
                else:
                    scores(off_next, tk, 0, 0)
                accumulate(off, tk, i, i % 2)
            return carry

        lax.fori_loop(0, n_chunks, body, 0)

    @pl.when(is_ctx)
    def _():
        for i in range(n_maps):
            scores(n_lat, t - n_lat, i, i % 2)
            accumulate(n_lat, t - n_lat, i, i % 2)

    lam = lam_ref[0]
    outs = []
    for h in range(DA_HEADS):
        a0, a1 = acc_scr[2 * h], acc_scr[2 * h + 1]
        o = (a0[0:DA_V_DIM] / a0[DA_V_DIM:DA_V_DIM + 1]
             - lam * (a1[0:DA_V_DIM] / a1[DA_V_DIM:DA_V_DIM + 1]))
        inv = lax.rsqrt(jnp.mean(o * o, axis=0, keepdims=True) + NORM_EPS)
        outs.append(o * inv * gain_ref[...] * fin_scale)
    o_ref[...] = jnp.concatenate(outs, axis=0).T.astype(BF16)


def _diff_attention(z, qt, vta, lam, gain, n_lat, fin_scale):
    b, t, _ = z.shape
    tq = ATT_BLOCK
    tk = max(w for w in (ATT_BLOCK, 2 * ATT_BLOCK, 3 * ATT_BLOCK) if t % w == 0)
    n_maps = 2 * DA_HEADS
    va_rows = vta.shape[2]
    return pl.pallas_call(
        functools.partial(_diff_attn_kernel, n_lat=n_lat, t=t, tq=tq, tk=tk, fin_scale=fin_scale),
        grid=(b, t // tq),
        in_specs=[pl.BlockSpec(memory_space=pltpu.SMEM),
                  pl.BlockSpec((None, GROUP, tq), lambda bi, m: (bi, 0, m)),
                  pl.BlockSpec((None, t, GROUP), lambda bi, m: (bi, 0, 1)),
                  pl.BlockSpec((None, DA_HEADS, va_rows, t), lambda bi, m: (bi, 0, 0, 0)),
                  pl.BlockSpec((DA_V_DIM, tq), lambda bi, m: (0, 0))],
        out_specs=pl.BlockSpec((None, tq, GROUP), lambda bi, m: (bi, m, 0)),
        out_shape=jax.ShapeDtypeStruct((b, t, GROUP), BF16),
        scratch_shapes=[pltpu.VMEM((n_maps, GROUP, tq), BF16), pltpu.VMEM((2, tk, tq), F32),
                        pltpu.VMEM((2, 1, tq), F32), pltpu.VMEM((n_maps, 1, tq), F32),
                        pltpu.VMEM((n_maps, va_rows, tq), F32)],
        compiler_params=_cparams(2),
    )(lam, qt, z, vta, gain)


def _retention_kernel(q_ref, k_ref, v_ref, g_ref, dm_ref, qw_ref, kw_ref, cd_ref, bd_ref, o_ref,
                      st_scr, of_scr, *, c, n_chunks, n_lat_chunks):
    d = pl.program_id(1)
    i = pl.program_id(2)
    chunk = jnp.where(d == 0, (i + n_lat_chunks) % n_chunks, n_chunks - 1 - i)

    @pl.when(i == 0)
    def _():
        st_scr[...] = jnp.zeros(st_scr.shape, F32)

    q, k, v = q_ref[...], k_ref[...], v_ref[...]
    lane_head = lax.broadcasted_iota(jnp.int32, (1, GROUP), 1) // RET_DIM
    o = jnp.dot((q.astype(F32) * qw_ref[...]).astype(BF16), st_scr[...].astype(BF16),
                preferred_element_type=F32)
    for h in range(RET_HEADS):
        mh = lane_head == h
        qh = q * mh.astype(BF16)
        s = lax.dot_general(qh, k, (((1,), (1,)), ((), ())), preferred_element_type=F32) * dm_ref[h]
        o = o + jnp.where(mh, jnp.dot(s.astype(BF16), v, preferred_element_type=F32), 0.0)
    kwk = (k.astype(F32) * kw_ref[...]).astype(BF16)
    kv = lax.dot_general(kwk, v, (((0,), (0,)), ((), ())), preferred_element_type=F32)
    st_scr[...] = cd_ref[...] * st_scr[...] + bd_ref[...] * kv
    off = pl.multiple_of(chunk * c, c)

    @pl.when(d == 0)
    def _():
        of_scr[pl.ds(off, c), :] = o

    @pl.when(d == 1)
    def _():
        tot = o + of_scr[pl.ds(off, c), :]
        sq = tot * tot
        inv = jnp.zeros(tot.shape, F32)
        for h in range(RET_HEADS):
            mh = lane_head == h
            ms = jnp.sum(jnp.where(mh, sq, 0.0), axis=-1, keepdims=True) * (1.0 / RET_DIM)
            inv = jnp.where(mh, lax.rsqrt(ms + NORM_EPS), inv)
        o_ref[...] = (tot * inv * jax.nn.silu(g_ref[...].astype(F32))).astype(BF16)


def _retention_tables(decay_logit, c):
    log_g = jax.nn.log_sigmoid(decay_logit.astype(F32))
    pos = jnp.arange(c, dtype=F32)
    rel = pos[:, None] - pos[None, :]
    rel = jnp.stack([rel, -rel])
    dm = jnp.where(rel[:, None] >= 0, jnp.exp(log_g[:, :, None, None] * jnp.maximum(rel[:, None], 0.0)), 0.0)
    lg_lane = jnp.repeat(log_g, RET_DIM, axis=1)
    q_exp = jnp.stack([pos + 1.0, c - pos])
    k_exp = jnp.stack([c - 1.0 - pos, pos])
    qw = jnp.exp(lg_lane[:, None, :] * q_exp[:, :, None])
    kw = jnp.exp(lg_lane[:, None, :] * k_exp[:, :, None])
    head = np.arange(GROUP) // RET_DIM
    bd = jnp.asarray((head[:, None] == head[None, :]).astype(np.float32))
    cd = jnp.exp(lg_lane * c)[:, :, None] * jnp.ones((1, 1, GROUP), F32)
    return dm, qw, kw, cd, bd


def _retention(z, decay_logit, n_lat):
    b, t, _ = z.shape
    c = ATT_BLOCK
    n_chunks, n_lat_chunks = t // c, n_lat // c
    dm, qw, kw, cd, bd = _retention_tables(decay_logit, c)

    def chunk_of(d, i):
        return jnp.where(d == 0, (i + n_lat_chunks) % n_chunks, n_chunks - 1 - i)

    def zspec(col):
        return pl.BlockSpec((None, c, GROUP), lambda bi, d, i: (bi, chunk_of(d, i), col))

    return pl.pallas_call(
        functools.partial(_retention_kernel, c=c, n_chunks=n_chunks, n_lat_chunks=n_lat_chunks),
        grid=(b, 2, n_chunks),
        in_specs=[zspec(3), zspec(4), zspec(5), zspec(6),
                  pl.BlockSpec((None, RET_HEADS, c, c), lambda bi, d, i: (d, 0, 0, 0)),
                  pl.BlockSpec((None, c, GROUP), lambda bi, d, i: (d, 0, 0)),
                  pl.BlockSpec((None, c, GROUP), lambda bi, d, i: (d, 0, 0)),
                  pl.BlockSpec((None, GROUP, GROUP), lambda bi, d, i: (d, 0, 0)),
                  pl.BlockSpec((GROUP, GROUP), lambda bi, d, i: (0, 0))],
        out_specs=pl.BlockSpec((None, c, GROUP),
                               lambda bi, d, i: (bi, jnp.where(d == 0, n_chunks - 1, n_chunks - 1 - i), 0)),
        out_shape=jax.ShapeDtypeStruct((b, t, GROUP), BF16),
        scratch_shapes=[pltpu.VMEM((GROUP, GROUP), F32), pltpu.VMEM((t, GROUP), F32)],
        compiler_params=_cparams(3),
    )(z, z, z, z, dm, qw, kw, cd, bd)


def _nbr_attn_kernel(q_ref, kp_ref, kc_ref, kn_ref, kx_ref, vp_ref, vc_ref, vn_ref, vx_ref, tab_ref, o_ref):
    q = q_ref[...]
    k = jnp.concatenate([kp_ref[...], kc_ref[...], kn_ref[...], kx_ref[...]], axis=0)
    v = jnp.concatenate([vp_ref[...], vc_ref[...], vn_ref[...], vx_ref[...]], axis=0)
    lane_head = lax.broadcasted_iota(jnp.int32, (1, GROUP), 1) // NA_DIM
    o = jnp.zeros(q.shape, F32)
    for h in range(NA_HEADS):
        mh = lane_head == h
        qh = q * mh.astype(BF16)
        s = lax.dot_general(qh, k, (((1,), (1,)), ((), ())), preferred_element_type=F32) + tab_ref[h]
        p = jnp.exp(s - jnp.max(s, axis=-1, keepdims=True))
        l = jnp.sum(p, axis=-1, keepdims=True)
        o = o + jnp.where(mh, jnp.dot(p.astype(BF16), v, preferred_element_type=F32) / l, 0.0)
    o_ref[...] = o.astype(BF16)


def _nbr_tables(rpb, n_lat, n_ctx):
    blk_rows = ATT_BLOCK // GRID_W
    rows = n_lat // GRID_W
    n_blk = rows // blk_rows
    assert rows >= NA_WIN_ROWS and n_blk >= 3
    qcol = kcol = np.arange(GRID_W)
    c0 = np.clip(qcol - NA_WIN_COLS // 2, 0, GRID_W - NA_WIN_COLS)
    col_ok = (kcol[None, :] >= c0[:, None]) & (kcol[None, :] < c0[:, None] + NA_WIN_COLS)
    dc = np.clip(kcol[None, :] - qcol[:, None] + (NA_WIN_COLS - 1), 0, 2 * NA_WIN_COLS - 2)
    oh_c = np.eye(2 * NA_WIN_COLS - 1, dtype=np.float32)[dc]
    qr, kr = np.arange(blk_rows), np.arange(3 * blk_rows)
    oh_r, oks = [], []
    for m in (0, 1, n_blk - 1):
        qrow, krow = blk_rows * m + qr, blk_rows * (m - 1) + kr
        r0 = np.clip(qrow - NA_WIN_ROWS // 2, 0, rows - NA_WIN_ROWS)
        row_ok = (krow[None, :] >= r0[:, None]) & (krow[None, :] < r0[:, None] + NA_WIN_ROWS)
        dr = np.clip(krow[None, :] - qrow[:, None] + (NA_WIN_ROWS - 1), 0, 2 * NA_WIN_ROWS - 2)
        oh_r.append(np.eye(2 * NA_WIN_ROWS - 1, dtype=np.float32)[dr])
        oks.append((row_ok[:, None, :, None] & col_ok[None, :, None, :]).reshape(ATT_BLOCK, 3 * ATT_BLOCK))
    bias = jnp.einsum('vqka,hab,QKb->vhqQkK', jnp.asarray(np.stack(oh_r)), rpb.astype(F32), jnp.asarray(oh_c),
                      precision=lax.Precision.HIGHEST).reshape(3, NA_HEADS, ATT_BLOCK, 3 * ATT_BLOCK)
    tab = jnp.where(jnp.asarray(np.stack(oks))[:, None], bias, NEG_BIG)
    tab = jnp.concatenate([tab, jnp.full((1, NA_HEADS, ATT_BLOCK, 3 * ATT_BLOCK), NEG_BIG, F32)], axis=0)
    return jnp.concatenate([tab, jnp.zeros((4, NA_HEADS, ATT_BLOCK, n_ctx), F32)], axis=-1)


def _nbr_attention(z, rpb, n_lat):
    b, t, _ = z.shape
    n_ctx = t - n_lat
    assert n_ctx == ATT_BLOCK and n_lat % ATT_BLOCK == 0
    n_blk = n_lat // ATT_BLOCK
    tab = _nbr_tables(rpb, n_lat, n_ctx)

    def kv_specs(col):
        mk = lambda f: pl.BlockSpec((None, ATT_BLOCK, GROUP), lambda bi, m: (bi, f(m), col))
        return [mk(lambda m: jnp.clip(m - 1, 0, n_blk - 1)), mk(lambda m: jnp.minimum(m, n_blk - 1)),
                mk(lambda m: jnp.clip(m + 1, 0, n_blk - 1)), mk(lambda m: n_blk)]

    def variant(m):
        return jnp.where(m == 0, 0, jnp.where(m == n_blk - 1, 2, jnp.where(m == n_blk, 3, 1)))

    win = 3 * ATT_BLOCK + n_ctx
    return pl.pallas_call(
        _nbr_attn_kernel,
        grid=(b, n_blk + 1),
        in_specs=[pl.BlockSpec((None, ATT_BLOCK, GROUP), lambda bi, m: (bi, m, 7))] + kv_specs(8) + kv_specs(9)
                 + [pl.BlockSpec((None, NA_HEADS, ATT_BLOCK, win), lambda bi, m: (variant(m), 0, 0, 0))],
        out_specs=pl.BlockSpec((None, ATT_BLOCK, GROUP), lambda bi, m: (bi, m, 0)),
        out_shape=jax.ShapeDtypeStruct((b, t, GROUP), BF16),
        compiler_params=_cparams(2),
    )(*([z] * 9), tab)


def _s5_kernel(u_ref, x0_ref, mm_ref, sm_ref, rm_ref, ap_ref, y_ref, xf_ref, st_scr, sloc_scr, xin_scr,
               *, rb):
    d = pl.program_id(0)
    i = pl.program_id(2)

    @pl.when(i == 0)
    def _():
        st_scr[...] = x0_ref[...]

    u = u_ref[...]
    sloc_scr[...] = jnp.dot(u, sm_ref[...], preferred_element_type=F32)
    ar, ai = ap_ref[0:1, :], ap_ref[1:2, :]
    ns = S5_NSTATE

    def step(n, carry):
        xr, xi = carry
        r = jnp.where(d == 0, n, rb - 1 - n)
        xin_scr[pl.ds(r, 1), 0:ns] = xr
        xin_scr[pl.ds(r, 1), ns:2 * ns] = xi
        s = sloc_scr[pl.ds(r, 1), :]
        return ar * xr - ai * xi + s[:, 0:ns], ar * xi + ai * xr + s[:, ns:2 * ns]

    xr, xi = lax.fori_loop(0, rb, step, (st_scr[:, 0:ns], st_scr[:, ns:2 * ns]))
    st_scr[:, 0:ns] = xr
    st_scr[:, ns:2 * ns] = xi
    y_ref[...] = (jnp.dot(u, mm_ref[...], preferred_element_type=F32)
                  + jnp.dot(xin_scr[...].astype(BF16), rm_ref[...], preferred_element_type=F32))
    xf_ref[...] = st_scr[...]


def _s5_operators(a_re, a_im, b_re, b_im, c_re, c_im, log_step):
    tc, g, p, ch = S5_CHUNK, S5_GROUPS, S5_STATE, S5_CH
    hi = lax.Precision.HIGHEST
    ar, ai = a_re.astype(F32), a_im.astype(F32)
    dt = jnp.exp(log_step.astype(F32))[..., None]
    steps = jnp.arange(tc + 1, dtype=F32)[None, :, None, None]
    mag = jnp.exp((ar * dt)[:, None] * steps)
    ang = (ai * dt)[:, None] * steps
    pw_r, pw_i = mag * jnp.cos(ang), mag * jnp.sin(ang)
    x, y, den = pw_r[:, 1] - 1.0, pw_i[:, 1], ar * ar + ai * ai
    cf_r, cf_i = ((x * ar + y * ai) / den)[..., None], ((y * ar - x * ai) / den)[..., None]
    bre, bim = b_re.astype(F32), b_im.astype(F32)
    bb_r, bb_i = cf_r * bre - cf_i * bim, cf_r * bim + cf_i * bre
    cr, ci = c_re.astype(F32), c_im.astype(F32)
    w_r = pw_r[..., None] * bb_r[:, None] - pw_i[..., None] * bb_i[:, None]
    w_i = pw_r[..., None] * bb_i[:, None] + pw_i[..., None] * bb_r[:, None]
    kern = (jnp.einsum('dgcp,dtgph->dtghc', cr, w_r, precision=hi)
            - jnp.einsum('dgcp,dtgph->dtghc', ci, w_i, precision=hi))
    eye = jnp.eye(g, dtype=F32)
    idx = np.arange(tc)
    outs = []
    for d in range(2):
        lag = (idx[None, :] - idx[:, None]) if d == 0 else (idx[:, None] - idx[None, :])
        kt = jnp.where(jnp.asarray(lag >= 0)[:, :, None, None, None],
                       kern[d][jnp.asarray(np.maximum(lag, 0))], 0.0)
        mm = (jnp.transpose(kt, (0, 2, 3, 1, 4))[:, :, :, :, None, :]
              * eye[None, :, None, None, :, None]).reshape(tc * g * ch, tc * g * ch)
        f_exp = jnp.asarray((tc - 1 - idx) if d == 0 else idx)
        sm = jnp.concatenate(
            [(jnp.transpose(part[d][f_exp], (0, 1, 3, 2))[:, :, :, None, :]
              * eye[None, :, None, :, None]).reshape(tc * g * ch, g * p) for part in (w_r, w_i)], axis=1)
        e_exp = jnp.asarray((idx + 1) if d == 0 else (tc - idx))
        pr, pi_ = pw_r[d][e_exp][:, :, None, :], pw_i[d][e_exp][:, :, None, :]
        ca_r, ca_i = cr[d][None] * pr - ci[d][None] * pi_, cr[d][None] * pi_ + ci[d][None] * pr
        rm = jnp.concatenate(
            [(jnp.transpose(part, (1, 3, 0, 2))[:, :, :, None, :]
              * eye[:, None, None, :, None]).reshape(g * p, tc * g * ch) for part in (ca_r, -ca_i)], axis=0)
        ap = jnp.stack([pw_r[d, tc].reshape(-1), pw_i[d, tc].reshape(-1)])
        outs.append((mm, sm, rm, ap))
    mm, sm, rm, ap = (jnp.stack(v) for v in zip(*outs))
    return mm.astype(BF16), sm.astype(BF16), rm.astype(BF16), ap


def _s5_scan(u_rows, row0, r, x0, ops, rb):
    b = u_rows.shape[0]
    nb = r // rb
    assert r % rb == 0 and row0 % rb == 0
    mm, sm, rm, ap = ops
    blk = lambda d, i: jnp.where(d == 0, i, nb - 1 - i)
    op_spec = lambda shp: pl.BlockSpec((None,) + shp, lambda d, bi, i: (d, 0, 0), pipeline_mode=pl.Buffered(1))
    return pl.pallas_call(
        functools.partial(_s5_kernel, rb=rb),
        grid=(2, b, nb),
        in_specs=[pl.BlockSpec((None, rb, S5_ROW), lambda d, bi, i: (bi, row0 // rb + blk(d, i), 0)),
                  pl.BlockSpec((None, None, 1, 2 * S5_NSTATE), lambda d, bi, i: (d, bi, 0, 0)),
                  op_spec((S5_ROW, S5_ROW)), op_spec((S5_ROW, 2 * S5_NSTATE)),
                  op_spec((2 * S5_NSTATE, S5_ROW)),
                  pl.BlockSpec((None, 2, S5_NSTATE), lambda d, bi, i: (d, 0, 0))],
        out_specs=[pl.BlockSpec((None, None, rb, S5_ROW), lambda d, bi, i: (d, bi, blk(d, i), 0)),
                   pl.BlockSpec((None, None, 1, 2 * S5_NSTATE), lambda d, bi, i: (d, bi, 0, 0))],
        out_shape=[jax.ShapeDtypeStruct((2, b, r, S5_ROW), F32),
                   jax.ShapeDtypeStruct((2, b, 1, 2 * S5_NSTATE), F32)],
        scratch_shapes=[pltpu.VMEM((1, 2 * S5_NSTATE), F32), pltpu.VMEM((rb, 2 * S5_NSTATE), F32),
                        pltpu.VMEM((rb, 2 * S5_NSTATE), F32)],
        compiler_params=_cparams(3),
    )(u_rows, x0, mm, sm, rm, ap)


def _s5_glu_kernel(y_ref, u_ref, d_ref, w_ref, b_ref, o_ref, y_scr, *, rb):
    yf = y_ref[0] + y_ref[1]
    for r in range(S5_CHUNK):
        for hf in range(GROUP // 128):
            lo = r * GROUP + hf * 128
            y_scr[hf, pl.ds(r, rb, stride=S5_CHUNK), :] = yf[:, lo:lo + 128]
    y = (u_ref[...].astype(F32) * d_ref[...]
         + jnp.concatenate([y_scr[hf] for hf in range(GROUP // 128)], axis=-1))
    act = jax.nn.gelu(y)
    gate = jnp.dot(act.astype(BF16), w_ref[...], preferred_element_type=F32) + b_ref[...]
    o_ref[...] = (act * jax.nn.sigmoid(gate)).astype(BF16)


def _s5_glu(y, z, tok0, rb, d_skip, glu_w, glu_b):
    b, r = y.shape[1], y.shape[2]
    tm = rb * S5_CHUNK
    assert r % rb == 0 and tok0 % tm == 0
    return pl.pallas_call(
        functools.partial(_s5_glu_kernel, rb=rb),
        grid=(b, r // rb),
        in_specs=[pl.BlockSpec((2, None, rb, S5_ROW), lambda bi, m: (0, bi, m, 0)),
                  pl.BlockSpec((None, tm, GROUP), lambda bi, m: (bi, tok0 // tm + m, 10)),
                  pl.BlockSpec((1, GROUP), lambda bi, m: (0, 0)),
                  pl.BlockSpec((GROUP, GROUP), lambda bi, m: (0, 0)),
                  pl.BlockSpec((1, GROUP), lambda bi, m: (0, 0))],
        out_specs=pl.BlockSpec((None, tm, GROUP), lambda bi, m: (bi, m, 0)),
        out_shape=jax.ShapeDtypeStruct((b, r * S5_CHUNK, GROUP), BF16),
        scratch_shapes=[pltpu.VMEM((GROUP // 128, tm, 128), F32)],
        compiler_params=_cparams(2),
    )(y, z, d_skip, glu_w, glu_b)


def _s5_mixer(z, u_rows, ops, d_skip, glu_w, glu_b, n_lat):
    b, t, _ = z.shape
    r_lat, r_ctx = n_lat // S5_CHUNK, (t - n_lat) // S5_CHUNK
    zero = jnp.zeros((2, b, 1, 2 * S5_NSTATE), F32)
    y_ctx, x_ctx = _s5_scan(u_rows, r_lat, r_ctx, zero, ops, r_ctx)
    y_lat, _ = _s5_scan(u_rows, 0, r_lat, x_ctx, ops, _pick_tile(r_lat, 256))
    glu = (d_skip.reshape(1, GROUP).astype(F32), glu_w.astype(BF16), glu_b.reshape(1, GROUP).astype(F32))
    s_lat = _s5_glu(y_lat, z, 0, _pick_tile(r_lat, 128), *glu)
    s_ctx = _s5_glu(y_ctx, z, n_lat, r_ctx, *glu)
    return jnp.concatenate([s_lat, s_ctx], axis=1)


def _out_proj_kernel(a_ref, r_ref, n_ref, s_ref, w_ref, x_ref, mod_ref, gain_ref, o_ref, *, n_lat, tm, d):
    m = pl.program_id(1)
    mix = jnp.dot(a_ref[...], w_ref[0:GROUP, :], preferred_element_type=F32)
    for j, ref in enumerate((r_ref, n_ref, s_ref), start=1):
        mix = mix + jnp.dot(ref[...], w_ref[j * GROUP:(j + 1) * GROUP, :], preferred_element_type=F32)
    gate = _row_mod(mod_ref, 2, m, tm, n_lat, d)
    o_ref[...] = x_ref[...] + gate * (_rms_rows(mix) * gain_ref[...])


def _out_proj(parts, w_bf, x, mod, gain, n_lat):
    b, t, d = x.shape
    tm = _pick_tile(t, 528)
    part_spec = pl.BlockSpec((None, tm, GROUP), lambda bi, m: (bi, m, 0))
    row_spec = pl.BlockSpec((None, tm, d), lambda bi, m: (bi, m, 0))
    return pl.pallas_call(
        functools.partial(_out_proj_kernel, n_lat=n_lat, tm=tm, d=d),
        grid=(b, t // tm),
        in_specs=[part_spec] * 4 + [pl.BlockSpec((d, d), lambda bi, m: (0, 0)), row_spec,
                                    pl.BlockSpec((None, 2, mod.shape[-1]), lambda bi, m: (bi, 0, 0)),
                                    pl.BlockSpec((1, d), lambda bi, m: (0, 0))],
        out_specs=row_spec,
        out_shape=jax.ShapeDtypeStruct((b, t, d), F32),
        compiler_params=_cparams(2),
    )(*parts, w_bf, x, mod, gain)


def _ffn_norm_kernel(x_ref, mod_ref, gain_ref, *rest, n_lat, tm, d, routed):
    m = pl.program_id(1)
    shift = _row_mod(mod_ref, 3, m, tm, n_lat, d)
    scale = _row_mod(mod_ref, 4, m, tm, n_lat, d)
    h = (_rms_rows(x_ref[...]) * gain_ref[...]) * (1.0 + scale) + shift
    if not routed:
        (h_ref,) = rest
        h_ref[...] = h.astype(BF16)
        return
    router_ref, h_ref, route_ref = rest
    h_ref[...] = h.astype(BF16)
    logits = jnp.dot(h, router_ref[...], precision=lax.Precision.HIGHEST, preferred_element_type=F32)
    lane = lax.broadcasted_iota(jnp.int32, logits.shape, 1)
    logits = jnp.where(lane < MOE_EXPERTS, logits, -jnp.inf)
    v1 = jnp.max(logits, axis=-1, keepdims=True)
    i1 = jnp.min(jnp.where(logits == v1, lane, 128), axis=-1, keepdims=True)
    rest_l = jnp.where(lane == i1, -jnp.inf, logits)
    v2 = jnp.max(rest_l, axis=-1, keepdims=True)
    i2 = jnp.min(jnp.where(rest_l == v2, lane, 128), axis=-1, keepdims=True)
    e = jnp.exp(v2 - v1)
    w1 = 1.0 / (1.0 + e)
    w2 = e / (1.0 + e)
    route = jnp.where(lane == 0, w1, jnp.where(lane == 1, w2, jnp.where(
        lane == 2, i1.astype(F32), jnp.where(lane == 3, i2.astype(F32), 0.0))))
    route_ref[...] = route


def _ffn_norm(x, mod, gain, n_lat, router=None):
    b, t, d = x.shape
    tm = _pick_tile(t, 528)
    routed = router is not None
    row_spec = pl.BlockSpec((None, tm, d), lambda bi, m: (bi, m, 0))
    in_specs = [row_spec, pl.BlockSpec((None, 2, mod.shape[-1]), lambda bi, m: (bi, 0, 0)),
                pl.BlockSpec((1, d), lambda bi, m: (0, 0))]
    out_specs, out_shape, args = [row_spec], [jax.ShapeDtypeStruct((b, t, d), BF16)], [x, mod, gain]
    if routed:
        in_specs.append(pl.BlockSpec((d, 128), lambda bi, m: (0, 0)))
        out_specs.append(pl.BlockSpec((None, tm, 128), lambda bi, m: (bi, m, 0)))
        out_shape.append(jax.ShapeDtypeStruct((b, t, 128), F32))
        args.append(jnp.pad(router.astype(F32), ((0, 0), (0, 128 - MOE_EXPERTS))))
    return pl.pallas_call(
        functools.partial(_ffn_norm_kernel, n_lat=n_lat, tm=tm, d=d, routed=routed),
        grid=(b, t // tm), in_specs=in_specs, out_specs=out_specs, out_shape=out_shape,
        compiler_params=_cparams(2),
    )(*args)


def _swiglu_kernel(te_ref, nu_ref, x_ref, wg_ref, wu_ref, wd_ref, o_ref, acc_scr):
    tile = pl.program_id(0)
    f = pl.program_id(1)
    nf = pl.num_programs(1)
    used = tile < nu_ref[0]

    @pl.when(f == 0)
    def _():
        acc_scr[...] = jnp.zeros(acc_scr.shape, F32)

    @pl.when(used)
    def _():
        x = x_ref[...]
        g = jnp.dot(x, wg_ref[...], preferred_element_type=F32)
        u = jnp.dot(x, wu_ref[...], preferred_element_type=F32)
        acc_scr[...] += jnp.dot((jax.nn.silu(g) * u).astype(BF16), wd_ref[...], preferred_element_type=F32)

    @pl.when(f == nf - 1)
    def _():
        o_ref[...] = acc_scr[...].astype(BF16)


def _swiglu_grouped(xs, tile_expert, n_used, w_gate, w_up, w_down, tm):
    p, d = xs.shape
    n_exp, _, f = w_gate.shape
    tf = f // 2
    grid_spec = pltpu.PrefetchScalarGridSpec(
        num_scalar_prefetch=2,
        grid=(p // tm, f // tf),
        in_specs=[pl.BlockSpec((tm, d), lambda t, j, te, nu: (t, 0)),
                  pl.BlockSpec((None, d, tf), lambda t, j, te, nu: (te[t], 0, j)),
                  pl.BlockSpec((None, d, tf), lambda t, j, te, nu: (te[t], 0, j)),
                  pl.BlockSpec((None, tf, d), lambda t, j, te, nu: (te[t], j, 0))],
        out_specs=pl.BlockSpec((tm, d), lambda t, j, te, nu: (t, 0)),
        scratch_shapes=[pltpu.VMEM((tm, d), F32)])
    return pl.pallas_call(
        _swiglu_kernel, grid_spec=grid_spec,
        out_shape=jax.ShapeDtypeStruct((p, d), BF16),
        compiler_params=_cparams(2),
    )(tile_expert, n_used, xs, w_gate, w_up, w_down)


def _ffn_residual_kernel(*refs, n_lat, tm, d, routed):
    m = pl.program_id(1)
    if routed:
        y1_ref, y2_ref, route_ref, x_ref, mod_ref, gain_ref, o_ref = refs
        y = (route_ref[:, 0:1] * y1_ref[...].astype(F32) + route_ref[:, 1:2] * y2_ref[...].astype(F32))
    else:
        y_ref, x_ref, mod_ref, gain_ref, o_ref = refs
        y = y_ref[...].astype(F32)
    gate = _row_mod(mod_ref, 5, m, tm, n_lat, d)
    o_ref[...] = x_ref[...] + gate * (_rms_rows(y) * gain_ref[...])


def _ffn_residual(ys, x, mod, gain, n_lat, route=None):
    b, t, d = x.shape
    tm = _pick_tile(t, 528)
    routed = route is not None
    row_spec = pl.BlockSpec((None, tm, d), lambda bi, m: (bi, m, 0))
    in_specs = [row_spec] * len(ys)
    if routed:
        in_specs.append(pl.BlockSpec((None, tm, 128), lambda bi, m: (bi, m, 0)))
    in_specs += [row_spec, pl.BlockSpec((None, 2, mod.shape[-1]), lambda bi, m: (bi, 0, 0)),
                 pl.BlockSpec((1, d), lambda bi, m: (0, 0))]
    return pl.pallas_call(
        functools.partial(_ffn_residual_kernel, n_lat=n_lat, tm=tm, d=d, routed=routed),
        grid=(b, t // tm), in_specs=in_specs, out_specs=row_spec,
        out_shape=jax.ShapeDtypeStruct((b, t, d), F32),
        compiler_params=_cparams(2),
    )(*ys, *((route,) if routed else ()), x, mod, gain)


def _dense_ffn(h, w_gate, w_up, w_down):
    b, t, d = h.shape
    n = b * t
    tm = _pick_tile(n, 528)
    out = _swiglu_grouped(h.reshape(n, d), jnp.zeros((n // tm,), jnp.int32),
                          jnp.full((1,), n // tm, jnp.int32), w_gate[None], w_up[None], w_down[None], tm)
    return out.reshape(b, t, d)


def _moe_ffn(h, route, w_gate, w_up, w_down):
    b, t, d = h.shape
    n = b * t
    tm = _pick_tile(2 * n, 512)
    route = route.reshape(n, 128)
    expert = jnp.concatenate([route[:, 2], route[:, 3]]).astype(jnp.int32)
    token = jnp.concatenate([jnp.arange(n, dtype=jnp.int32)] * 2)
    onehot = (expert[:, None] == jnp.arange(MOE_EXPERTS, dtype=jnp.int32)[None, :]).astype(jnp.int32)
    counts = jnp.sum(onehot, axis=0)
    rank = jnp.sum((jnp.cumsum(onehot, axis=0) - onehot) * onehot, axis=1)
    padded = ((counts + tm - 1) // tm) * tm
    ends = jnp.cumsum(padded)
    pos = (ends - padded)[expert] + rank
    n_rows = 2 * n + MOE_EXPERTS * tm
    n_tiles = n_rows // tm
    tile_start = jnp.arange(n_tiles, dtype=jnp.int32) * tm
    tile_expert = jnp.minimum(jnp.sum((ends[None, :] <= tile_start[:, None]).astype(jnp.int32), axis=1),
                              MOE_EXPERTS - 1)
    n_used = (ends[-1] // tm).astype(jnp.int32).reshape(1)
    row_token = jnp.zeros((n_rows,), jnp.int32).at[pos].set(token)
    xs = jnp.take(h.reshape(n, d), row_token, axis=0)
    out = _swiglu_grouped(xs, tile_expert, n_used, w_gate, w_up, w_down, tm)
    return (jnp.take(out, pos[:n], axis=0).reshape(b, t, d), jnp.take(out, pos[n:], axis=0).reshape(b, t, d))


def kernel(x, c, ctx, c_ctx, w_mod, b_mod, norm_gain, w_in, w_out, da_lambda, da_norm_gain,
           ret_decay_logit, na_rpb, s5_a_re, s5_a_im, s5_b_re, s5_b_im, s5_c_re, s5_c_im,
           s5_log_step, s5_d, s5_glu_w, s5_glu_b, ffn_w_gate, ffn_w_up, ffn_w_down,
           moe_router, moe_w_gate, moe_w_up, moe_w_down):
    b, n_lat, d = x.shape
    n_ctx = ctx.shape[1]
    t = n_lat + n_ctx
    depth = w_mod.shape[0]
    assert d == 4 * GROUP and b + 1 <= 8

    c_rows = jnp.zeros((8, d), F32).at[:b].set(c).at[b].set(c_ctx)
    mod_all = _modulation(c_rows, w_mod, b_mod)
    tabs = _rope_tables(n_lat, t, DA_QK_DIM) + _rope_tables(n_lat, t, RET_DIM)
    h_all = jnp.concatenate([x, ctx], axis=1)

    for layer in range(depth):
        mod = jnp.stack([mod_all[layer, :b], jnp.broadcast_to(mod_all[layer, b], (b, 6 * d))], axis=1)
        g = norm_gain[layer].astype(F32)
        lam_init = 0.8 - 0.6 * math.exp(-0.3 * layer)

        z, qt, vta, u_rows = _in_proj(h_all, mod, g[0:1], w_in[layer].astype(BF16), tabs, n_lat)
        lp = da_lambda[layer].astype(F32)
        lam = (jnp.exp(jnp.sum(lp[0] * lp[1])) - jnp.exp(jnp.sum(lp[2] * lp[3])) + lam_init).reshape(1)
        da_gain = jnp.broadcast_to(da_norm_gain[layer].astype(F32)[:, None], (DA_V_DIM, ATT_BLOCK))
        a_out = _diff_attention(z, qt, vta, lam, da_gain, n_lat, 1.0 - lam_init)
        r_out = _retention(z, ret_decay_logit[layer], n_lat)
        n_out = _nbr_attention(z, na_rpb[layer], n_lat)
        ops = _s5_operators(s5_a_re[layer], s5_a_im[layer], s5_b_re[layer], s5_b_im[layer],
                            s5_c_re[layer], s5_c_im[layer], s5_log_step[layer])
        s_out = _s5_mixer(z, u_rows, ops, s5_d[layer], s5_glu_w[layer], s5_glu_b[layer], n_lat)
        h_all = _out_proj((a_out, r_out, n_out, s_out), w_out[layer].astype(BF16), h_all, mod, g[1:2], n_lat)

        i = layer // 2
        if layer % 2 == 0:
            (f_in,) = _ffn_norm(h_all, mod, g[2:3], n_lat)
            f_out = _dense_ffn(f_in, ffn_w_gate[i].astype(BF16), ffn_w_up[i].astype(BF16),
                               ffn_w_down[i].astype(BF16))
            h_all = _ffn_residual((f_out,), h_all, mod, g[3:4], n_lat)
        else:
            f_in, route = _ffn_norm(h_all, mod, g[2:3], n_lat, router=moe_router[i])
            f_outs = _moe_ffn(f_in, route, moe_w_gate[i].astype(BF16), moe_w_up[i].astype(BF16),
                              moe_w_down[i].astype(BF16))
            h_all = _ffn_residual(f_outs, h_all, mod, g[3:4], n_lat, route=route)
    return h_all[:, :n_lat]
```

```python
import functools
import math

import numpy as np
import jax
import jax.numpy as jnp
from jax import lax
from jax.experimental import pallas as pl
from jax.experimental.pallas import tpu as pltpu

F32 = jnp.float32
BF16 = jnp.bfloat16

GRID_W = 64
GROUP = 256
N_IN_SLICES = 11
DA_HEADS, DA_QK_DIM, DA_V_DIM = 4, 32, 64
DA_ONES_ROWS = 16
DA_MAX_KEY_CHUNK = 2816
RET_HEADS, RET_DIM = 4, 64
NA_HEADS, NA_DIM, NA_WIN_ROWS, NA_WIN_COLS = 4, 64, 8, 16
S5_CH, S5_GROUPS, S5_STATE = 16, 16, 64
MOE_EXPERTS, MOE_TOP_K = 8, 2
ROPE_BASE = 10000.0
NORM_EPS = 1e-6
NEG_BIG = -1e30

S5_CHUNK = 8
S5_ROW = S5_CHUNK * GROUP
S5_NSTATE = S5_GROUPS * S5_STATE
ATT_BLOCK = 256
VMEM_LIMIT = 48 * 1024 * 1024


def _cparams(n_axes):
    return pltpu.CompilerParams(dimension_semantics=("arbitrary",) * n_axes,
                                vmem_limit_bytes=VMEM_LIMIT)


def _pick_tile(n, target, mult=16):
    best = None
    for t in range(mult, min(n, target) + 1, mult):
        if n % t == 0:
            best = t
    assert best is not None, (n, target)
    return best


def _rms_rows(x):
    return x * lax.rsqrt(jnp.mean(x * x, axis=-1, keepdims=True) + NORM_EPS)


def _row_mod(mod_ref, k, m, tm, n_lat, d):
    row = m * tm + lax.broadcasted_iota(jnp.int32, (tm, 1), 0)
    return jnp.where(row >= n_lat, mod_ref[1:2, k * d:(k + 1) * d], mod_ref[0:1, k * d:(k + 1) * d])


def _mod_kernel(c_ref, w_ref, b_ref, o_ref):
    o_ref[...] = jnp.dot(jax.nn.silu(c_ref[...]), w_ref[...], precision=lax.Precision.HIGHEST,
                         preferred_element_type=F32) + b_ref[...]


def _modulation(c_rows, w_mod, b_mod):
    depth, d, n6 = w_mod.shape
    tn = 1024
    return pl.pallas_call(
        _mod_kernel,
        grid=(depth, n6 // tn),
        in_specs=[pl.BlockSpec((8, d), lambda l, n: (0, 0)),
                  pl.BlockSpec((None, d, tn), lambda l, n: (l, 0, n)),
                  pl.BlockSpec((None, 1, tn), lambda l, n: (l, 0, n))],
        out_specs=pl.BlockSpec((None, 8, tn), lambda l, n: (l, 0, n)),
        out_shape=jax.ShapeDtypeStruct((depth, 8, n6), F32),
        compiler_params=_cparams(2),
    )(c_rows, w_mod, b_mod.reshape(depth, 1, n6))


def _rope(z, cos, sin_signed, q):
    n = z.shape[-1]
    lane = lax.broadcasted_iota(jnp.int32, z.shape, 1)
    first = (lane % (2 * q)) < q
    partner = jnp.where(first, pltpu.roll(z, n - q, 1), pltpu.roll(z, q, 1))
    return z * cos + partner * sin_signed


def _in_proj_kernel(x_ref, mod_ref, gain_ref, w_ref, cda_ref, sda_ref, crt_ref, srt_ref,
                    z_ref, qt_ref, vta_ref, uf_ref, u_scr, *, n_lat, tm, d):
    m = pl.program_id(1)
    shift = _row_mod(mod_ref, 0, m, tm, n_lat, d)
    scale = _row_mod(mod_ref, 1, m, tm, n_lat, d)
    h = ((_rms_rows(x_ref[...]) * gain_ref[...]) * (1.0 + scale) + shift).astype(BF16)
    col_scale = {0: DA_QK_DIM ** -0.5 * math.log2(math.e), 4: RET_DIM ** -0.5, 7: NA_DIM ** -0.5}
    for j in range(N_IN_SLICES):
        zj = jnp.dot(h, w_ref[:, j * GROUP:(j + 1) * GROUP], preferred_element_type=F32)
        if j in (0, 1):
            zj = _rope(zj, cda_ref[...], sda_ref[...], DA_QK_DIM // 4)
        elif j in (3, 4):
            zj = _rope(zj, crt_ref[...], srt_ref[...], RET_DIM // 4)
        if j in col_scale:
            zj = zj * col_scale[j]
        z_ref[:, j * GROUP:(j + 1) * GROUP] = zj.astype(BF16)
        if j == 0:
            qt_ref[...] = zj.T.astype(BF16)
        elif j == 2:
            vta_ref[:, 0:DA_V_DIM, :] = zj.T.reshape(DA_HEADS, DA_V_DIM, tm).astype(BF16)
            vta_ref[:, DA_V_DIM:, :] = jnp.ones((DA_HEADS, DA_ONES_ROWS, tm), BF16)
        elif j == 10:
            for hf in range(GROUP // 128):
                u_scr[hf] = zj[:, hf * 128:(hf + 1) * 128]
            for r in range(S5_CHUNK):
                for hf in range(GROUP // 128):
                    lo = r * GROUP + hf * 128
                    uf_ref[:, lo:lo + 128] = u_scr[hf, pl.ds(r, tm // S5_CHUNK, stride=S5_CHUNK), :].astype(BF16)


def _in_proj(x, mod, gain, w_bf, tabs, n_lat):
    b, t, d = x.shape
    n_out = w_bf.shape[1]
    tm = _pick_tile(t, 768, mult=128)
    tab_spec = pl.BlockSpec((tm, GROUP), lambda bi, m: (m, 0))
    va_rows = DA_V_DIM + DA_ONES_ROWS
    return pl.pallas_call(
        functools.partial(_in_proj_kernel, n_lat=n_lat, tm=tm, d=d),
        grid=(b, t // tm),
        in_specs=[pl.BlockSpec((None, tm, d), lambda bi, m: (bi, m, 0)),
                  pl.BlockSpec((None, 2, mod.shape[-1]), lambda bi, m: (bi, 0, 0)),
                  pl.BlockSpec((1, d), lambda bi, m: (0, 0)),
                  pl.BlockSpec((d, n_out), lambda bi, m: (0, 0)),
                  tab_spec, tab_spec, tab_spec, tab_spec],
        out_specs=[pl.BlockSpec((None, tm, n_out), lambda bi, m: (bi, m, 0)),
                   pl.BlockSpec((None, GROUP, tm), lambda bi, m: (bi, 0, m)),
                   pl.BlockSpec((None, DA_HEADS, va_rows, tm), lambda bi, m: (bi, 0, 0, m)),
                   pl.BlockSpec((None, tm // S5_CHUNK, S5_ROW), lambda bi, m: (bi, m, 0))],
        out_shape=[jax.ShapeDtypeStruct((b, t, n_out), BF16),
                   jax.ShapeDtypeStruct((b, GROUP, t), BF16),
                   jax.ShapeDtypeStruct((b, DA_HEADS, va_rows, t), BF16),
                   jax.ShapeDtypeStruct((b, t // S5_CHUNK, S5_ROW), BF16)],
        scratch_shapes=[pltpu.VMEM((GROUP // 128, tm, 128), F32)],
        compiler_params=_cparams(2),
    )(x, mod, gain, w_bf, *tabs)


def _rope_tables(n_lat, t, dim):
    q = dim // 4
    tok = jnp.arange(n_lat)
    row = (tok // GRID_W).astype(F32)
    col = (tok % GRID_W).astype(F32)
    inv = ROPE_BASE ** (-jnp.arange(q, dtype=F32) / q)
    lane = np.arange(GROUP) % dim
    axis, half, qi = lane // (2 * q), (lane % (2 * q)) // q, lane % q
    pos = jnp.where(jnp.asarray(axis)[None, :] == 0, row[:, None], col[:, None])
    ang = pos * inv[jnp.asarray(qi)][None, :]
    sign = jnp.asarray(np.where(half == 0, -1.0, 1.0), F32)[None, :]
    cos = jnp.concatenate([jnp.cos(ang), jnp.ones((t - n_lat, GROUP), F32)], axis=0)
    sin = jnp.concatenate([jnp.sin(ang) * sign, jnp.zeros((t - n_lat, GROUP), F32)], axis=0)
    return cos, sin


def _diff_attn_kernel(lam_ref, qt_ref, k_ref, vta_ref, gain_ref, o_ref, qm_scr, s_scr, mx_scr, m_scr, acc_scr,
                      *, n_lat, t, tq, tk, fin_scale):
    qi = pl.program_id(1)
    n_maps = 2 * DA_HEADS
    m_scr[...] = jnp.full(m_scr.shape, -jnp.inf, F32)
    acc_scr[...] = jnp.zeros(acc_scr.shape, F32)
    qt = qt_ref[...]
    row_map = lax.broadcasted_iota(jnp.int32, (GROUP, 1), 0) // DA_QK_DIM
    for i in range(n_maps):
        qm_scr[i] = qt * (row_map == i).astype(BF16)

    def scores(off, width, i, slot):
        s = jnp.dot(k_ref[pl.ds(off, width), :], qm_scr[i], preferred_element_type=F32)
        s_scr[slot, 0:width, :] = s
        mx_scr[slot] = jnp.max(s, axis=0, keepdims=True)

    def accumulate(off, width, i, slot):
        m_prev = m_scr[i]
        m_next = jnp.maximum(m_prev, mx_scr[slot])
        p = jnp.exp2(s_scr[slot, 0:width, :] - m_next).astype(BF16)
        va = vta_ref[i // 2, :, pl.ds(off, width)]
        acc_scr[i] = jnp.exp2(m_prev - m_next) * acc_scr[i] + jnp.dot(va, p, preferred_element_type=F32)
        m_scr[i] = m_next

    is_ctx = qi * tq >= n_lat
    n_chunks = t // tk

    @pl.when(jnp.logical_not(is_ctx))
    def _():
        scores(0, tk, 0, 0)

        def body(c, carry):
            off = pl.multiple_of(c * tk, tk)
            off_next = pl.multiple_of(jnp.minimum(c + 1, n_chunks - 1) * tk, tk)
            for i in range(n_maps):
                if i + 1 < n_maps:
                    scores(off, tk, i + 1, (i + 1) % 2)
                else:
                    scores(off_next, tk, 0, 0)
                accumulate(off, tk, i, i % 2)
            return carry

        lax.fori_loop(0, n_chunks, body, 0)

    @pl.when(is_ctx)
    def _():
        for i in range(n_maps):
            scores(n_lat, t - n_lat, i, i % 2)
            accumulate(n_lat, t - n_lat, i, i % 2)

    lam = lam_ref[0]
    outs = []
    for h in range(DA_HEADS):
        a0, a1 = acc_scr[2 * h], acc_scr[2 * h + 1]
        o = (a0[0:DA_V_DIM] / a0[DA_V_DIM:DA_V_DIM + 1]
             - lam * (a1[0:DA_V_DIM] / a1[DA_V_DIM:DA_V_DIM + 1]))
        inv = lax.rsqrt(jnp.mean(o * o, axis=0, keepdims=True) + NORM_EPS)
        outs.append(o * inv * gain_ref[...] * fin_scale)
    o_ref[...] = jnp.concatenate(outs, axis=0).T.astype(BF16)


def _diff_attention(z, qt, vta, lam, gain, n_lat, fin_scale):
    b, t, _ = z.shape
    tq = ATT_BLOCK
    tk = max(w for w in range(ATT_BLOCK, DA_MAX_KEY_CHUNK + 1, ATT_BLOCK) if t % w == 0)
    n_maps = 2 * DA_HEADS
    va_rows = vta.shape[2]
    return pl.pallas_call(
        functools.partial(_diff_attn_kernel, n_lat=n_lat, t=t, tq=tq, tk=tk, fin_scale=fin_scale),
        grid=(b, t // tq),
        in_specs=[pl.BlockSpec(memory_space=pltpu.SMEM),
                  pl.BlockSpec((None, GROUP, tq), lambda bi, m: (bi, 0, m)),
                  pl.BlockSpec((None, t, GROUP), lambda bi, m: (bi, 0, 1)),
                  pl.BlockSpec((None, DA_HEADS, va_rows, t), lambda bi, m: (bi, 0, 0, 0)),
                  pl.BlockSpec((DA_V_DIM, tq), lambda bi, m: (0, 0))],
        out_specs=pl.BlockSpec((None, tq, GROUP), lambda bi, m: (bi, m, 0)),
        out_shape=jax.ShapeDtypeStruct((b, t, GROUP), BF16),
        scratch_shapes=[pltpu.VMEM((n_maps, GROUP, tq), BF16), pltpu.VMEM((2, tk, tq), F32),
                        pltpu.VMEM((2, 1, tq), F32), pltpu.VMEM((n_maps, 1, tq), F32),
                        pltpu.VMEM((n_maps, va_rows, tq), F32)],
        compiler_params=_cparams(2),
    )(lam, qt, z, vta, gain)


def _retention_kernel(q_ref, k_ref, v_ref, g_ref, dm_ref, qw_ref, kw_ref, cd_ref, bd_ref, o_ref,
                      st_scr, of_scr, *, c, n_chunks, n_lat_chunks):
    d = pl.program_id(1)
    i = pl.program_id(2)
    chunk = jnp.where(d == 0, (i + n_lat_chunks) % n_chunks, n_chunks - 1 - i)

    @pl.when(i == 0)
    def _():
        st_scr[...] = jnp.zeros(st_scr.shape, F32)

    q, k, v = q_ref[...], k_ref[...], v_ref[...]
    lane_head = lax.broadcasted_iota(jnp.int32, (1, GROUP), 1) // RET_DIM
    o = jnp.dot((q.astype(F32) * qw_ref[...]).astype(BF16), st_scr[...].astype(BF16),
                preferred_element_type=F32)
    for h in range(RET_HEADS):
        mh = lane_head == h
        qh = q * mh.astype(BF16)
        s = lax.dot_general(qh, k, (((1,), (1,)), ((), ())), preferred_element_type=F32) * dm_ref[h]
        o = o + jnp.where(mh, jnp.dot(s.astype(BF16), v, preferred_element_type=F32), 0.0)
    kwk = (k.astype(F32) * kw_ref[...]).astype(BF16)
    kv = lax.dot_general(kwk, v, (((0,), (0,)), ((), ())), preferred_element_type=F32)
    st_scr[...] = cd_ref[...] * st_scr[...] + bd_ref[...] * kv
    off = pl.multiple_of(chunk * c, c)

    @pl.when(d == 0)
    def _():
        of_scr[pl.ds(off, c), :] = o

    @pl.when(d == 1)
    def _():
        tot = o + of_scr[pl.ds(off, c), :]
        sq = tot * tot
        inv = jnp.zeros(tot.shape, F32)
        for h in range(RET_HEADS):
            mh = lane_head == h
            ms = jnp.sum(jnp.where(mh, sq, 0.0), axis=-1, keepdims=True) * (1.0 / RET_DIM)
            inv = jnp.where(mh, lax.rsqrt(ms + NORM_EPS), inv)
        o_ref[...] = (tot * inv * jax.nn.silu(g_ref[...].astype(F32))).astype(BF16)


def _retention_tables(decay_logit, c):
    log_g = jax.nn.log_sigmoid(decay_logit.astype(F32))
    pos = jnp.arange(c, dtype=F32)
    rel = pos[:, None] - pos[None, :]
    rel = jnp.stack([rel, -rel])
    dm = jnp.where(rel[:, None] >= 0, jnp.exp(log_g[:, :, None, None] * jnp.maximum(rel[:, None], 0.0)), 0.0)
    lg_lane = jnp.repeat(log_g, RET_DIM, axis=1)
    q_exp = jnp.stack([pos + 1.0, c - pos])
    k_exp = jnp.stack([c - 1.0 - pos, pos])
    qw = jnp.exp(lg_lane[:, None, :] * q_exp[:, :, None])
    kw = jnp.exp(lg_lane[:, None, :] * k_exp[:, :, None])
    head = np.arange(GROUP) // RET_DIM
    bd = jnp.asarray((head[:, None] == head[None, :]).astype(np.float32))
    cd = jnp.exp(lg_lane * c)[:, :, None] * jnp.ones((1, 1, GROUP), F32)
    return dm, qw, kw, cd, bd


def _retention(z, tables, layer, n_lat):
    b, t, _ = z.shape
    c = ATT_BLOCK
    n_chunks, n_lat_chunks = t // c, n_lat // c
    dm, qw, kw, cd, bd = tables

    def chunk_of(d, i):
        return jnp.where(d == 0, (i + n_lat_chunks) % n_chunks, n_chunks - 1 - i)

    def zspec(col):
        return pl.BlockSpec((None, c, GROUP), lambda bi, d, i: (bi, chunk_of(d, i), col))

    return pl.pallas_call(
        functools.partial(_retention_kernel, c=c, n_chunks=n_chunks, n_lat_chunks=n_lat_chunks),
        grid=(b, 2, n_chunks),
        in_specs=[zspec(3), zspec(4), zspec(5), zspec(6),
                  pl.BlockSpec((None, None, RET_HEADS, c, c), lambda bi, d, i: (layer, d, 0, 0, 0)),
                  pl.BlockSpec((None, None, c, GROUP), lambda bi, d, i: (layer, d, 0, 0)),
                  pl.BlockSpec((None, None, c, GROUP), lambda bi, d, i: (layer, d, 0, 0)),
                  pl.BlockSpec((None, None, GROUP, GROUP), lambda bi, d, i: (layer, d, 0, 0)),
                  pl.BlockSpec((None, GROUP, GROUP), lambda bi, d, i: (layer, 0, 0))],
        out_specs=pl.BlockSpec((None, c, GROUP),
                               lambda bi, d, i: (bi, jnp.where(d == 0, n_chunks - 1, n_chunks - 1 - i), 0)),
        out_shape=jax.ShapeDtypeStruct((b, t, GROUP), BF16),
        scratch_shapes=[pltpu.VMEM((GROUP, GROUP), F32), pltpu.VMEM((t, GROUP), F32)],
        compiler_params=_cparams(3),
    )(z, z, z, z, dm, qw, kw, cd, bd)


def _nbr_attn_kernel(q_ref, kp_ref, kc_ref, kn_ref, kx_ref, vp_ref, vc_ref, vn_ref, vx_ref, tab_ref, o_ref):
    q = q_ref[...]
    k = jnp.concatenate([kp_ref[...], kc_ref[...], kn_ref[...], kx_ref[...]], axis=0)
    v = jnp.concatenate([vp_ref[...], vc_ref[...], vn_ref[...], vx_ref[...]], axis=0)
    lane_head = lax.broadcasted_iota(jnp.int32, (1, GROUP), 1) // NA_DIM
    o = jnp.zeros(q.shape, F32)
    for h in range(NA_HEADS):
        mh = lane_head == h
        qh = q * mh.astype(BF16)
        s = lax.dot_general(qh, k, (((1,), (1,)), ((), ())), preferred_element_type=F32) + tab_ref[h]
        p = jnp.exp(s - jnp.max(s, axis=-1, keepdims=True))
        l = jnp.sum(p, axis=-1, keepdims=True)
        o = o + jnp.where(mh, jnp.dot(p.astype(BF16), v, preferred_element_type=F32) / l, 0.0)
    o_ref[...] = o.astype(BF16)


def _nbr_tables(rpb, n_lat, n_ctx):
    blk_rows = ATT_BLOCK // GRID_W
    rows = n_lat // GRID_W
    n_blk = rows // blk_rows
    assert rows >= NA_WIN_ROWS and n_blk >= 3
    qcol = kcol = np.arange(GRID_W)
    c0 = np.clip(qcol - NA_WIN_COLS // 2, 0, GRID_W - NA_WIN_COLS)
    col_ok = (kcol[None, :] >= c0[:, None]) & (kcol[None, :] < c0[:, None] + NA_WIN_COLS)
    dc = np.clip(kcol[None, :] - qcol[:, None] + (NA_WIN_COLS - 1), 0, 2 * NA_WIN_COLS - 2)
    oh_c = np.eye(2 * NA_WIN_COLS - 1, dtype=np.float32)[dc]
    qr, kr = np.arange(blk_rows), np.arange(3 * blk_rows)
    oh_r, oks = [], []
    for m in (0, 1, n_blk - 1):
        qrow, krow = blk_rows * m + qr, blk_rows * (m - 1) + kr
        r0 = np.clip(qrow - NA_WIN_ROWS // 2, 0, rows - NA_WIN_ROWS)
        row_ok = (krow[None, :] >= r0[:, None]) & (krow[None, :] < r0[:, None] + NA_WIN_ROWS)
        dr = np.clip(krow[None, :] - qrow[:, None] + (NA_WIN_ROWS - 1), 0, 2 * NA_WIN_ROWS - 2)
        oh_r.append(np.eye(2 * NA_WIN_ROWS - 1, dtype=np.float32)[dr])
        oks.append((row_ok[:, None, :, None] & col_ok[None, :, None, :]).reshape(ATT_BLOCK, 3 * ATT_BLOCK))
    bias = jnp.einsum('vqka,hab,QKb->vhqQkK', jnp.asarray(np.stack(oh_r)), rpb.astype(F32), jnp.asarray(oh_c),
                      precision=lax.Precision.HIGHEST).reshape(3, NA_HEADS, ATT_BLOCK, 3 * ATT_BLOCK)
    tab = jnp.where(jnp.asarray(np.stack(oks))[:, None], bias, NEG_BIG)
    tab = jnp.concatenate([tab, jnp.full((1, NA_HEADS, ATT_BLOCK, 3 * ATT_BLOCK), NEG_BIG, F32)], axis=0)
    return jnp.concatenate([tab, jnp.zeros((4, NA_HEADS, ATT_BLOCK, n_ctx), F32)], axis=-1)


def _nbr_attention(z, tab, layer, n_lat):
    b, t, _ = z.shape
    n_ctx = t - n_lat
    assert n_ctx == ATT_BLOCK and n_lat % ATT_BLOCK == 0
    n_blk = n_lat // ATT_BLOCK

    def kv_specs(col):
        mk = lambda f: pl.BlockSpec((None, ATT_BLOCK, GROUP), lambda bi, m: (bi, f(m), col))
        return [mk(lambda m: jnp.clip(m - 1, 0, n_blk - 1)), mk(lambda m: jnp.minimum(m, n_blk - 1)),
                mk(lambda m: jnp.clip(m + 1, 0, n_blk - 1)), mk(lambda m: n_blk)]

    def variant(m):
        return jnp.where(m == 0, 0, jnp.where(m == n_blk - 1, 2, jnp.where(m == n_blk, 3, 1)))

    win = 3 * ATT_BLOCK + n_ctx
    return pl.pallas_call(
        _nbr_attn_kernel,
        grid=(b, n_blk + 1),
        in_specs=[pl.BlockSpec((None, ATT_BLOCK, GROUP), lambda bi, m: (bi, m, 7))] + kv_specs(8) + kv_specs(9)
                 + [pl.BlockSpec((None, None, NA_HEADS, ATT_BLOCK, win),
                                 lambda bi, m: (layer, variant(m), 0, 0, 0))],
        out_specs=pl.BlockSpec((None, ATT_BLOCK, GROUP), lambda bi, m: (bi, m, 0)),
        out_shape=jax.ShapeDtypeStruct((b, t, GROUP), BF16),
        compiler_params=_cparams(2),
    )(*([z] * 9), tab)


def _s5_kernel(u_ref, x0_ref, kb_ref, sm_ref, rm_ref, ap_ref, y_ref, xf_ref, st_scr, sloc_scr, xin_scr,
               *, rb):
    d = pl.program_id(0)
    i = pl.program_id(2)

    @pl.when(i == 0)
    def _():
        st_scr[...] = x0_ref[...]

    u = u_ref[...]
    sloc_scr[...] = jnp.dot(u, sm_ref[...], preferred_element_type=F32)
    ar, ai = ap_ref[0:1, :], ap_ref[1:2, :]
    ns = S5_NSTATE

    def step(n, carry):
        xr, xi = carry
        r = jnp.where(d == 0, n, rb - 1 - n)
        xin_scr[pl.ds(r, 1), 0:ns] = xr
        xin_scr[pl.ds(r, 1), ns:2 * ns] = xi
        s = sloc_scr[pl.ds(r, 1), :]
        return ar * xr - ai * xi + s[:, 0:ns], ar * xi + ai * xr + s[:, ns:2 * ns]

    xr, xi = lax.fori_loop(0, rb, step, (st_scr[:, 0:ns], st_scr[:, ns:2 * ns]))
    st_scr[:, 0:ns] = xr
    st_scr[:, ns:2 * ns] = xi
    y_state = jnp.dot(xin_scr[...].astype(BF16), rm_ref[...], preferred_element_type=F32)
    for direction in range(2):
        @pl.when(d == direction)
        def _():
            for t_out in range(S5_CHUNK):
                srcs = range(0, t_out + 1) if direction == 0 else range(t_out, S5_CHUNK)
                acc = y_state[:, t_out * GROUP:(t_out + 1) * GROUP]
                for t_in in srcs:
                    acc = acc + jnp.dot(u[:, t_in * GROUP:(t_in + 1) * GROUP], kb_ref[abs(t_out - t_in)],
                                        preferred_element_type=F32)
                y_ref[:, t_out * GROUP:(t_out + 1) * GROUP] = acc
    xf_ref[...] = st_scr[...]


def _s5_operators(a_re, a_im, b_re, b_im, c_re, c_im, log_step):
    tc, g, p, ch = S5_CHUNK, S5_GROUPS, S5_STATE, S5_CH
    hi = lax.Precision.HIGHEST
    ar, ai = a_re.astype(F32), a_im.astype(F32)
    dt = jnp.exp(log_step.astype(F32))[..., None]
    steps = jnp.arange(tc + 1, dtype=F32)[None, :, None, None]
    mag = jnp.exp((ar * dt)[:, None] * steps)
    ang = (ai * dt)[:, None] * steps
    pw_r, pw_i = mag * jnp.cos(ang), mag * jnp.sin(ang)
    x, y, den = pw_r[:, 1] - 1.0, pw_i[:, 1], ar * ar + ai * ai
    cf_r, cf_i = ((x * ar + y * ai) / den)[..., None], ((y * ar - x * ai) / den)[..., None]
    bre, bim = b_re.astype(F32), b_im.astype(F32)
    bb_r, bb_i = cf_r * bre - cf_i * bim, cf_r * bim + cf_i * bre
    cr, ci = c_re.astype(F32), c_im.astype(F32)
    w_r = pw_r[..., None] * bb_r[:, None] - pw_i[..., None] * bb_i[:, None]
    w_i = pw_r[..., None] * bb_i[:, None] + pw_i[..., None] * bb_r[:, None]
    kern = (jnp.einsum('dgcp,dtgph->dtghc', cr, w_r, precision=hi)
            - jnp.einsum('dgcp,dtgph->dtghc', ci, w_i, precision=hi))
    eye = jnp.eye(g, dtype=F32)
    idx = np.arange(tc)
    kb = (kern[:, 0:tc, :, :, None, :] * eye[None, None, :, None, :, None]).reshape(2, tc, g * ch, g * ch)
    outs = []
    for d in range(2):
        f_exp = jnp.asarray((tc - 1 - idx) if d == 0 else idx)
        sm = jnp.concatenate(
            [(jnp.transpose(part[d][f_exp], (0, 1, 3, 2))[:, :, :, None, :]
              * eye[None, :, None, :, None]).reshape(tc * g * ch, g * p) for part in (w_r, w_i)], axis=1)
        e_exp = jnp.asarray((idx + 1) if d == 0 else (tc - idx))
        pr, pi_ = pw_r[d][e_exp][:, :, None, :], pw_i[d][e_exp][:, :, None, :]
        ca_r, ca_i = cr[d][None] * pr - ci[d][None] * pi_, cr[d][None] * pi_ + ci[d][None] * pr
        rm = jnp.concatenate(
            [(jnp.transpose(part, (1, 3, 0, 2))[:, :, :, None, :]
              * eye[:, None, None, :, None]).reshape(g * p, tc * g * ch) for part in (ca_r, -ca_i)], axis=0)
        ap = jnp.stack([pw_r[d, tc].reshape(-1), pw_i[d, tc].reshape(-1)])
        outs.append((sm, rm, ap))
    sm, rm, ap = (jnp.stack(v) for v in zip(*outs))
    return kb.astype(BF16), sm.astype(BF16), rm.astype(BF16), ap


def _s5_scan(u_rows, row0, r, x0, ops, layer, rb):
    b = u_rows.shape[0]
    nb = r // rb
    assert r % rb == 0 and row0 % rb == 0
    kb, sm, rm, ap = ops
    blk = lambda d, i: jnp.where(d == 0, i, nb - 1 - i)
    op_spec = lambda shp: pl.BlockSpec((None, None) + shp, lambda d, bi, i: (layer, d) + (0,) * len(shp),
                                       pipeline_mode=pl.Buffered(1))
    return pl.pallas_call(
        functools.partial(_s5_kernel, rb=rb),
        grid=(2, b, nb),
        in_specs=[pl.BlockSpec((None, rb, S5_ROW), lambda d, bi, i: (bi, row0 // rb + blk(d, i), 0)),
                  pl.BlockSpec((None, None, 1, 2 * S5_NSTATE), lambda d, bi, i: (d, bi, 0, 0)),
                  op_spec((S5_CHUNK, GROUP, GROUP)), op_spec((S5_ROW, 2 * S5_NSTATE)),
                  op_spec((2 * S5_NSTATE, S5_ROW)),
                  pl.BlockSpec((None, None, 2, S5_NSTATE), lambda d, bi, i: (layer, d, 0, 0))],
        out_specs=[pl.BlockSpec((None, None, rb, S5_ROW), lambda d, bi, i: (d, bi, blk(d, i), 0)),
                   pl.BlockSpec((None, None, 1, 2 * S5_NSTATE), lambda d, bi, i: (d, bi, 0, 0))],
        out_shape=[jax.ShapeDtypeStruct((2, b, r, S5_ROW), F32),
                   jax.ShapeDtypeStruct((2, b, 1, 2 * S5_NSTATE), F32)],
        scratch_shapes=[pltpu.VMEM((1, 2 * S5_NSTATE), F32), pltpu.VMEM((rb, 2 * S5_NSTATE), F32),
                        pltpu.VMEM((rb, 2 * S5_NSTATE), F32)],
        compiler_params=_cparams(3),
    )(u_rows, x0, kb, sm, rm, ap)


def _s5_glu_kernel(y_ref, u_ref, d_ref, w_ref, b_ref, o_ref, y_scr, *, rb):
    yf = y_ref[0] + y_ref[1]
    for r in range(S5_CHUNK):
        for hf in range(GROUP // 128):
            lo = r * GROUP + hf * 128
            y_scr[hf, pl.ds(r, rb, stride=S5_CHUNK), :] = yf[:, lo:lo + 128]
    y = (u_ref[...].astype(F32) * d_ref[...]
         + jnp.concatenate([y_scr[hf] for hf in range(GROUP // 128)], axis=-1))
    act = jax.nn.gelu(y)
    gate = jnp.dot(act.astype(BF16), w_ref[...], preferred_element_type=F32) + b_ref[...]
    o_ref[...] = (act * jax.nn.sigmoid(gate)).astype(BF16)


def _s5_glu(y, z, tok0, rb, d_skip, glu_w, glu_b):
    b, r = y.shape[1], y.shape[2]
    tm = rb * S5_CHUNK
    assert r % rb == 0 and tok0 % tm == 0
    return pl.pallas_call(
        functools.partial(_s5_glu_kernel, rb=rb),
        grid=(b, r // rb),
        in_specs=[pl.BlockSpec((2, None, rb, S5_ROW), lambda bi, m: (0, bi, m, 0)),
                  pl.BlockSpec((None, tm, GROUP), lambda bi, m: (bi, tok0 // tm + m, 10)),
                  pl.BlockSpec((1, GROUP), lambda bi, m: (0, 0)),
                  pl.BlockSpec((GROUP, GROUP), lambda bi, m: (0, 0)),
                  pl.BlockSpec((1, GROUP), lambda bi, m: (0, 0))],
        out_specs=pl.BlockSpec((None, tm, GROUP), lambda bi, m: (bi, m, 0)),
        out_shape=jax.ShapeDtypeStruct((b, r * S5_CHUNK, GROUP), BF16),
        scratch_shapes=[pltpu.VMEM((GROUP // 128, tm, 128), F32)],
        compiler_params=_cparams(2),
    )(y, z, d_skip, glu_w, glu_b)


def _s5_mixer(z, u_rows, ops, layer, d_skip, glu_w, glu_b, n_lat):
    b, t, _ = z.shape
    r_lat, r_ctx = n_lat // S5_CHUNK, (t - n_lat) // S5_CHUNK
    zero = jnp.zeros((2, b, 1, 2 * S5_NSTATE), F32)
    y_ctx, x_ctx = _s5_scan(u_rows, r_lat, r_ctx, zero, ops, layer, r_ctx)
    y_lat, _ = _s5_scan(u_rows, 0, r_lat, x_ctx, ops, layer, _pick_tile(r_lat, 256))
    glu = (d_skip.reshape(1, GROUP).astype(F32), glu_w.astype(BF16), glu_b.reshape(1, GROUP).astype(F32))
    s_lat = _s5_glu(y_lat, z, 0, _pick_tile(r_lat, 128), *glu)
    s_ctx = _s5_glu(y_ctx, z, n_lat, r_ctx, *glu)
    return jnp.concatenate([s_lat, s_ctx], axis=1)


def _out_proj_kernel(a_ref, r_ref, n_ref, s_ref, w_ref, x_ref, mod_ref, gain_ref, o_ref, *, n_lat, tm, d):
    m = pl.program_id(1)
    mix = jnp.dot(a_ref[...], w_ref[0:GROUP, :], preferred_element_type=F32)
    for j, ref in enumerate((r_ref, n_ref, s_ref), start=1):
        mix = mix + jnp.dot(ref[...], w_ref[j * GROUP:(j + 1) * GROUP, :], preferred_element_type=F32)
    gate = _row_mod(mod_ref, 2, m, tm, n_lat, d)
    o_ref[...] = x_ref[...] + gate * (_rms_rows(mix) * gain_ref[...])


def _out_proj(parts, w_bf, x, mod, gain, n_lat):
    b, t, d = x.shape
    tm = _pick_tile(t, 528)
    part_spec = pl.BlockSpec((None, tm, GROUP), lambda bi, m: (bi, m, 0))
    row_spec = pl.BlockSpec((None, tm, d), lambda bi, m: (bi, m, 0))
    return pl.pallas_call(
        functools.partial(_out_proj_kernel, n_lat=n_lat, tm=tm, d=d),
        grid=(b, t // tm),
        in_specs=[part_spec] * 4 + [pl.BlockSpec((d, d), lambda bi, m: (0, 0)), row_spec,
                                    pl.BlockSpec((None, 2, mod.shape[-1]), lambda bi, m: (bi, 0, 0)),
                                    pl.BlockSpec((1, d), lambda bi, m: (0, 0))],
        out_specs=row_spec,
        out_shape=jax.ShapeDtypeStruct((b, t, d), F32),
        compiler_params=_cparams(2),
    )(*parts, w_bf, x, mod, gain)


def _moe_norm_kernel(x_ref, mod_ref, gain_ref, router_ref, h_ref, route_ref, *, n_lat, tm, d):
    m = pl.program_id(1)
    shift = _row_mod(mod_ref, 3, m, tm, n_lat, d)
    scale = _row_mod(mod_ref, 4, m, tm, n_lat, d)
    h = (_rms_rows(x_ref[...]) * gain_ref[...]) * (1.0 + scale) + shift
    h_ref[...] = h.astype(BF16)
    logits = jnp.dot(h, router_ref[...], precision=lax.Precision.HIGHEST, preferred_element_type=F32)
    lane = lax.broadcasted_iota(jnp.int32, logits.shape, 1)
    logits = jnp.where(lane < MOE_EXPERTS, logits, -jnp.inf)
    v1 = jnp.max(logits, axis=-1, keepdims=True)
    i1 = jnp.min(jnp.where(logits == v1, lane, 128), axis=-1, keepdims=True)
    rest_l = jnp.where(lane == i1, -jnp.inf, logits)
    v2 = jnp.max(rest_l, axis=-1, keepdims=True)
    i2 = jnp.min(jnp.where(rest_l == v2, lane, 128), axis=-1, keepdims=True)
    e = jnp.exp(v2 - v1)
    w1 = 1.0 / (1.0 + e)
    w2 = e / (1.0 + e)
    route = jnp.where(lane == 0, w1, jnp.where(lane == 1, w2, jnp.where(
        lane == 2, i1.astype(F32), jnp.where(lane == 3, i2.astype(F32), 0.0))))
    route_ref[...] = route


def _moe_norm(x, mod, gain, router, n_lat):
    b, t, d = x.shape
    tm = _pick_tile(t, 528)
    row_spec = pl.BlockSpec((None, tm, d), lambda bi, m: (bi, m, 0))
    return pl.pallas_call(
        functools.partial(_moe_norm_kernel, n_lat=n_lat, tm=tm, d=d),
        grid=(b, t // tm),
        in_specs=[row_spec, pl.BlockSpec((None, 2, mod.shape[-1]), lambda bi, m: (bi, 0, 0)),
                  pl.BlockSpec((1, d), lambda bi, m: (0, 0)), pl.BlockSpec((d, 128), lambda bi, m: (0, 0))],
        out_specs=[row_spec, pl.BlockSpec((None, tm, 128), lambda bi, m: (bi, m, 0))],
        out_shape=[jax.ShapeDtypeStruct((b, t, d), BF16), jax.ShapeDtypeStruct((b, t, 128), F32)],
        compiler_params=_cparams(2),
    )(x, mod, gain, jnp.pad(router.astype(F32), ((0, 0), (0, 128 - MOE_EXPERTS))))


def _swiglu_kernel(te_ref, nu_ref, x_ref, wg_ref, wu_ref, wd_ref, o_ref, acc_scr):
    tile = pl.program_id(0)
    f = pl.program_id(1)
    nf = pl.num_programs(1)
    used = tile < nu_ref[0]

    @pl.when(f == 0)
    def _():
        acc_scr[...] = jnp.zeros(acc_scr.shape, F32)

    @pl.when(used)
    def _():
        x = x_ref[...]
        g = jnp.dot(x, wg_ref[...], preferred_element_type=F32)
        u = jnp.dot(x, wu_ref[...], preferred_element_type=F32)
        acc_scr[...] += jnp.dot((jax.nn.silu(g) * u).astype(BF16), wd_ref[...], preferred_element_type=F32)

    @pl.when(f == nf - 1)
    def _():
        o_ref[...] = acc_scr[...].astype(BF16)


def _swiglu_grouped(xs, tile_expert, n_used, w_gate, w_up, w_down, tm):
    p, d = xs.shape
    n_exp, _, f = w_gate.shape
    tf = f // 2
    grid_spec = pltpu.PrefetchScalarGridSpec(
        num_scalar_prefetch=2,
        grid=(p // tm, f // tf),
        in_specs=[pl.BlockSpec((tm, d), lambda t, j, te, nu: (t, 0)),
                  pl.BlockSpec((None, d, tf), lambda t, j, te, nu: (te[t], 0, j)),
                  pl.BlockSpec((None, d, tf), lambda t, j, te, nu: (te[t], 0, j)),
                  pl.BlockSpec((None, tf, d), lambda t, j, te, nu: (te[t], j, 0))],
        out_specs=pl.BlockSpec((tm, d), lambda t, j, te, nu: (t, 0)),
        scratch_shapes=[pltpu.VMEM((tm, d), F32)])
    return pl.pallas_call(
        _swiglu_kernel, grid_spec=grid_spec,
        out_shape=jax.ShapeDtypeStruct((p, d), BF16),
        compiler_params=_cparams(2),
    )(tile_expert, n_used, xs, w_gate, w_up, w_down)


def _moe_residual_kernel(y1_ref, y2_ref, route_ref, x_ref, mod_ref, gain_ref, o_ref, *, n_lat, tm, d):
    m = pl.program_id(1)
    y = route_ref[:, 0:1] * y1_ref[...].astype(F32) + route_ref[:, 1:2] * y2_ref[...].astype(F32)
    gate = _row_mod(mod_ref, 5, m, tm, n_lat, d)
    o_ref[...] = x_ref[...] + gate * (_rms_rows(y) * gain_ref[...])


def _moe_residual(y1, y2, route, x, mod, gain, n_lat):
    b, t, d = x.shape
    tm = _pick_tile(t, 528)
    row_spec = pl.BlockSpec((None, tm, d), lambda bi, m: (bi, m, 0))
    return pl.pallas_call(
        functools.partial(_moe_residual_kernel, n_lat=n_lat, tm=tm, d=d),
        grid=(b, t // tm),
        in_specs=[row_spec, row_spec, pl.BlockSpec((None, tm, 128), lambda bi, m: (bi, m, 0)), row_spec,
                  pl.BlockSpec((None, 2, mod.shape[-1]), lambda bi, m: (bi, 0, 0)),
                  pl.BlockSpec((1, d), lambda bi, m: (0, 0))],
        out_specs=row_spec,
        out_shape=jax.ShapeDtypeStruct((b, t, d), F32),
        compiler_params=_cparams(2),
    )(y1, y2, route, x, mod, gain)


def _dense_ffn_kernel(x_ref, mod_ref, gin_ref, gout_ref, wg_ref, wu_ref, wd_ref, o_ref, h_scr, acc_scr,
                      *, n_lat, tm, d):
    m = pl.program_id(1)
    f = pl.program_id(2)

    @pl.when(f == 0)
    def _():
        shift = _row_mod(mod_ref, 3, m, tm, n_lat, d)
        scale = _row_mod(mod_ref, 4, m, tm, n_lat, d)
        h_scr[...] = ((_rms_rows(x_ref[...]) * gin_ref[...]) * (1.0 + scale) + shift).astype(BF16)
        acc_scr[...] = jnp.zeros(acc_scr.shape, F32)

    h = h_scr[...]
    g = jnp.dot(h, wg_ref[...], preferred_element_type=F32)
    u = jnp.dot(h, wu_ref[...], preferred_element_type=F32)
    acc_scr[...] += jnp.dot((jax.nn.silu(g) * u).astype(BF16), wd_ref[...], preferred_element_type=F32)

    @pl.when(f == pl.num_programs(2) - 1)
    def _():
        gate = _row_mod(mod_ref, 5, m, tm, n_lat, d)
        o_ref[...] = x_ref[...] + gate * (_rms_rows(acc_scr[...]) * gout_ref[...])


def _dense_ffn(x, mod, gain_in, gain_out, w_gate, w_up, w_down, n_lat):
    b, t, d = x.shape
    f = w_gate.shape[1]
    tm = _pick_tile(t, 528)
    tf = f // 2
    row_spec = pl.BlockSpec((None, tm, d), lambda bi, m, j: (bi, m, 0))
    vec_spec = pl.BlockSpec((1, d), lambda bi, m, j: (0, 0))
    return pl.pallas_call(
        functools.partial(_dense_ffn_kernel, n_lat=n_lat, tm=tm, d=d),
        grid=(b, t // tm, f // tf),
        in_specs=[row_spec, pl.BlockSpec((None, 2, mod.shape[-1]), lambda bi, m, j: (bi, 0, 0)),
                  vec_spec, vec_spec,
                  pl.BlockSpec((d, tf), lambda bi, m, j: (0, j)), pl.BlockSpec((d, tf), lambda bi, m, j: (0, j)),
                  pl.BlockSpec((tf, d), lambda bi, m, j: (j, 0))],
        out_specs=row_spec,
        out_shape=jax.ShapeDtypeStruct((b, t, d), F32),
        scratch_shapes=[pltpu.VMEM((tm, d), BF16), pltpu.VMEM((tm, d), F32)],
        compiler_params=_cparams(3),
    )(x, mod, gain_in, gain_out, w_gate, w_up, w_down)


def _moe_ffn(h, route, w_gate, w_up, w_down):
    b, t, d = h.shape
    n = b * t
    tm = _pick_tile(2 * n, 512)
    route = route.reshape(n, 128)
    expert = jnp.concatenate([route[:, 2], route[:, 3]]).astype(jnp.int32)
    token = jnp.concatenate([jnp.arange(n, dtype=jnp.int32)] * 2)
    onehot = (expert[:, None] == jnp.arange(MOE_EXPERTS, dtype=jnp.int32)[None, :]).astype(jnp.int32)
    counts = jnp.sum(onehot, axis=0)
    rank = jnp.sum((jnp.cumsum(onehot, axis=0) - onehot) * onehot, axis=1)
    padded = ((counts + tm - 1) // tm) * tm
    ends = jnp.cumsum(padded)
    pos = jnp.sum(onehot * (ends - padded)[None, :], axis=1) + rank
    n_rows = 2 * n + MOE_EXPERTS * tm
    n_tiles = n_rows // tm
    tile_start = jnp.arange(n_tiles, dtype=jnp.int32) * tm
    tile_expert = jnp.minimum(jnp.sum((ends[None, :] <= tile_start[:, None]).astype(jnp.int32), axis=1),
                              MOE_EXPERTS - 1)
    n_used = (ends[-1] // tm).astype(jnp.int32).reshape(1)
    row_token = jnp.zeros((n_rows,), jnp.int32).at[pos].set(token)
    xs = jnp.take(h.reshape(n, d), row_token, axis=0)
    out = _swiglu_grouped(xs, tile_expert, n_used, w_gate, w_up, w_down, tm)
    return (jnp.take(out, pos[:n], axis=0).reshape(b, t, d), jnp.take(out, pos[n:], axis=0).reshape(b, t, d))


def kernel(x, c, ctx, c_ctx, w_mod, b_mod, norm_gain, w_in, w_out, da_lambda, da_norm_gain,
           ret_decay_logit, na_rpb, s5_a_re, s5_a_im, s5_b_re, s5_b_im, s5_c_re, s5_c_im,
           s5_log_step, s5_d, s5_glu_w, s5_glu_b, ffn_w_gate, ffn_w_up, ffn_w_down,
           moe_router, moe_w_gate, moe_w_up, moe_w_down):
    b, n_lat, d = x.shape
    n_ctx = ctx.shape[1]
    t = n_lat + n_ctx
    depth = w_mod.shape[0]
    assert d == 4 * GROUP and b + 1 <= 8

    c_rows = jnp.zeros((8, d), F32).at[:b].set(c).at[b].set(c_ctx)
    mod_all = _modulation(c_rows, w_mod, b_mod)
    tabs = _rope_tables(n_lat, t, DA_QK_DIM) + _rope_tables(n_lat, t, RET_DIM)
    h_all = jnp.concatenate([x, ctx], axis=1)

    lam_inits = [0.8 - 0.6 * math.exp(-0.3 * layer) for layer in range(depth)]
    lp = da_lambda.astype(F32)
    lams = (jnp.exp(jnp.sum(lp[:, 0] * lp[:, 1], axis=-1)) - jnp.exp(jnp.sum(lp[:, 2] * lp[:, 3], axis=-1))
            + jnp.asarray(lam_inits, F32))
    ret_tabs = jax.vmap(lambda dl: _retention_tables(dl, ATT_BLOCK))(ret_decay_logit)
    na_tabs = jax.vmap(lambda r: _nbr_tables(r, n_lat, n_ctx))(na_rpb)
    s5_ops = jax.vmap(_s5_operators)(s5_a_re, s5_a_im, s5_b_re, s5_b_im, s5_c_re, s5_c_im, s5_log_step)

    for layer in range(depth):
        mod = jnp.stack([mod_all[layer, :b], jnp.broadcast_to(mod_all[layer, b], (b, 6 * d))], axis=1)
        g = norm_gain[layer].astype(F32)
        lam_init = lam_inits[layer]

        z, qt, vta, u_rows = _in_proj(h_all, mod, g[0:1], w_in[layer].astype(BF16), tabs, n_lat)
        da_gain = jnp.broadcast_to(da_norm_gain[layer].astype(F32)[:, None], (DA_V_DIM, ATT_BLOCK))
        a_out = _diff_attention(z, qt, vta, lams[layer].reshape(1), da_gain, n_lat, 1.0 - lam_init)
        r_out = _retention(z, ret_tabs, layer, n_lat)
        n_out = _nbr_attention(z, na_tabs, layer, n_lat)
        s_out = _s5_mixer(z, u_rows, s5_ops, layer, s5_d[layer], s5_glu_w[layer], s5_glu_b[layer], n_lat)
        h_all = _out_proj((a_out, r_out, n_out, s_out), w_out[layer].astype(BF16), h_all, mod, g[1:2], n_lat)

        i = layer // 2
        if layer % 2 == 0:
            h_all = _dense_ffn(h_all, mod, g[2:3], g[3:4], ffn_w_gate[i].astype(BF16), ffn_w_up[i].astype(BF16),
                               ffn_w_down[i].astype(BF16), n_lat)
        else:
            f_in, route = _moe_norm(h_all, mod, g[2:3], moe_router[i], n_lat)
            y1, y2 = _moe_ffn(f_in, route, moe_w_gate[i].astype(BF16), moe_w_up[i].astype(BF16),
                              moe_w_down[i].astype(BF16))
            h_all = _moe_residual(y1, y2, route, h_all, mod, g[3:4], n_lat)
    return h_all[:, :n_lat]
```

```python
import functools
import math

import numpy as np
import jax
import jax.numpy as jnp
from jax import lax
from jax.experimental import pallas as pl
from jax.experimental.pallas import tpu as pltpu

F32 = jnp.float32
BF16 = jnp.bfloat16

GRID_W = 64
GROUP = 256
N_IN_SLICES = 11
DA_HEADS, DA_QK_DIM, DA_V_DIM = 4, 32, 64
DA_ONES_ROWS = 16
DA_MAX_KEY_CHUNK = 4224
RET_HEADS, RET_DIM = 4, 64
NA_HEADS, NA_DIM, NA_WIN_ROWS, NA_WIN_COLS = 4, 64, 8, 16
S5_CH, S5_GROUPS, S5_STATE = 16, 16, 64
MOE_EXPERTS, MOE_TOP_K = 8, 2
ROPE_BASE = 10000.0
NORM_EPS = 1e-6
NEG_BIG = -1e30

S5_CHUNK = 8
S5_ROW = S5_CHUNK * GROUP
S5_NSTATE = S5_GROUPS * S5_STATE
ATT_BLOCK = 256
VMEM_LIMIT = 48 * 1024 * 1024


def _cparams(n_axes):
    return pltpu.CompilerParams(dimension_semantics=("arbitrary",) * n_axes,
                                vmem_limit_bytes=VMEM_LIMIT)


def _pick_tile(n, target, mult=16):
    best = None
    for t in range(mult, min(n, target) + 1, mult):
        if n % t == 0:
            best = t
    assert best is not None, (n, target)
    return best


def _rms_rows(x):
    return x * lax.rsqrt(jnp.mean(x * x, axis=-1, keepdims=True) + NORM_EPS)


def _row_mod(mod_ref, k, m, tm, n_lat, d):
    row = m * tm + lax.broadcasted_iota(jnp.int32, (tm, 1), 0)
    return jnp.where(row >= n_lat, mod_ref[1:2, k * d:(k + 1) * d], mod_ref[0:1, k * d:(k + 1) * d])


def _mod_kernel(c_ref, w_ref, b_ref, o_ref):
    o_ref[...] = jnp.dot(jax.nn.silu(c_ref[...]), w_ref[...], precision=lax.Precision.HIGHEST,
                         preferred_element_type=F32) + b_ref[...]


def _modulation(c_rows, w_mod, b_mod):
    depth, d, n6 = w_mod.shape
    tn = 1024
    return pl.pallas_call(
        _mod_kernel,
        grid=(depth, n6 // tn),
        in_specs=[pl.BlockSpec((8, d), lambda l, n: (0, 0)),
                  pl.BlockSpec((None, d, tn), lambda l, n: (l, 0, n)),
                  pl.BlockSpec((None, 1, tn), lambda l, n: (l, 0, n))],
        out_specs=pl.BlockSpec((None, 8, tn), lambda l, n: (l, 0, n)),
        out_shape=jax.ShapeDtypeStruct((depth, 8, n6), F32),
        compiler_params=_cparams(2),
    )(c_rows, w_mod, b_mod.reshape(depth, 1, n6))


def _rope(z, cos, sin_signed, q):
    n = z.shape[-1]
    lane = lax.broadcasted_iota(jnp.int32, z.shape, 1)
    first = (lane % (2 * q)) < q
    partner = jnp.where(first, pltpu.roll(z, n - q, 1), pltpu.roll(z, q, 1))
    return z * cos + partner * sin_signed


def _in_proj_kernel(x_ref, mod_ref, gain_ref, w_ref, cda_ref, sda_ref, crt_ref, srt_ref,
                    z_ref, qt_ref, vta_ref, uf_ref, u_scr, *, n_lat, tm, d):
    m = pl.program_id(1)
    shift = _row_mod(mod_ref, 0, m, tm, n_lat, d)
    scale = _row_mod(mod_ref, 1, m, tm, n_lat, d)
    h = ((_rms_rows(x_ref[...]) * gain_ref[...]) * (1.0 + scale) + shift).astype(BF16)
    col_scale = {0: DA_QK_DIM ** -0.5 * math.log2(math.e), 4: RET_DIM ** -0.5, 7: NA_DIM ** -0.5}
    for j in range(N_IN_SLICES):
        zj = jnp.dot(h, w_ref[:, j * GROUP:(j + 1) * GROUP], preferred_element_type=F32)
        if j in (0, 1):
            zj = _rope(zj, cda_ref[...], sda_ref[...], DA_QK_DIM // 4)
        elif j in (3, 4):
            zj = _rope(zj, crt_ref[...], srt_ref[...], RET_DIM // 4)
        if j in col_scale:
            zj = zj * col_scale[j]
        z_ref[:, j * GROUP:(j + 1) * GROUP] = zj.astype(BF16)
        if j == 0:
            qt_ref[...] = zj.T.astype(BF16)
        elif j == 2:
            vta_ref[:, 0:DA_V_DIM, :] = zj.T.reshape(DA_HEADS, DA_V_DIM, tm).astype(BF16)
            vta_ref[:, DA_V_DIM:, :] = jnp.ones((DA_HEADS, DA_ONES_ROWS, tm), BF16)
        elif j == 10:
            for hf in range(GROUP // 128):
                u_scr[hf] = zj[:, hf * 128:(hf + 1) * 128]
            for r in range(S5_CHUNK):
                for hf in range(GROUP // 128):
                    lo = r * GROUP + hf * 128
                    uf_ref[:, lo:lo + 128] = u_scr[hf, pl.ds(r, tm // S5_CHUNK, stride=S5_CHUNK), :].astype(BF16)


def _in_proj(x, mod, gain, w_bf, tabs, n_lat):
    b, t, d = x.shape
    n_out = w_bf.shape[1]
    tm = _pick_tile(t, 768, mult=128)
    tab_spec = pl.BlockSpec((tm, GROUP), lambda bi, m: (m, 0))
    va_rows = DA_V_DIM + DA_ONES_ROWS
    return pl.pallas_call(
        functools.partial(_in_proj_kernel, n_lat=n_lat, tm=tm, d=d),
        grid=(b, t // tm),
        in_specs=[pl.BlockSpec((None, tm, d), lambda bi, m: (bi, m, 0)),
                  pl.BlockSpec((None, 2, mod.shape[-1]), lambda bi, m: (bi, 0, 0)),
                  pl.BlockSpec((1, d), lambda bi, m: (0, 0)),
                  pl.BlockSpec((d, n_out), lambda bi, m: (0, 0)),
                  tab_spec, tab_spec, tab_spec, tab_spec],
        out_specs=[pl.BlockSpec((None, tm, n_out), lambda bi, m: (bi, m, 0)),
                   pl.BlockSpec((None, GROUP, tm), lambda bi, m: (bi, 0, m)),
                   pl.BlockSpec((None, DA_HEADS, va_rows, tm), lambda bi, m: (bi, 0, 0, m)),
                   pl.BlockSpec((None, tm // S5_CHUNK, S5_ROW), lambda bi, m: (bi, m, 0))],
        out_shape=[jax.ShapeDtypeStruct((b, t, n_out), BF16),
                   jax.ShapeDtypeStruct((b, GROUP, t), BF16),
                   jax.ShapeDtypeStruct((b, DA_HEADS, va_rows, t), BF16),
                   jax.ShapeDtypeStruct((b, t // S5_CHUNK, S5_ROW), BF16)],
        scratch_shapes=[pltpu.VMEM((GROUP // 128, tm, 128), F32)],
        compiler_params=_cparams(2),
    )(x, mod, gain, w_bf, *tabs)


def _rope_tables(n_lat, t, dim):
    q = dim // 4
    tok = jnp.arange(n_lat)
    row = (tok // GRID_W).astype(F32)
    col = (tok % GRID_W).astype(F32)
    inv = ROPE_BASE ** (-jnp.arange(q, dtype=F32) / q)
    lane = np.arange(GROUP) % dim
    axis, half, qi = lane // (2 * q), (lane % (2 * q)) // q, lane % q
    pos = jnp.where(jnp.asarray(axis)[None, :] == 0, row[:, None], col[:, None])
    ang = pos * inv[jnp.asarray(qi)][None, :]
    sign = jnp.asarray(np.where(half == 0, -1.0, 1.0), F32)[None, :]
    cos = jnp.concatenate([jnp.cos(ang), jnp.ones((t - n_lat, GROUP), F32)], axis=0)
    sin = jnp.concatenate([jnp.sin(ang) * sign, jnp.zeros((t - n_lat, GROUP), F32)], axis=0)
    return cos, sin


def _diff_attn_kernel(lam_ref, qt_ref, k_ref, vta_ref, gain_ref, o_ref, qm_scr, s_scr, mx_scr, m_scr, acc_scr,
                      *, n_lat, t, tq, tk, fin_scale):
    qi = pl.program_id(1)
    n_maps = 2 * DA_HEADS
    m_scr[...] = jnp.full(m_scr.shape, -jnp.inf, F32)
    acc_scr[...] = jnp.zeros(acc_scr.shape, F32)
    qt = qt_ref[...]
    row_map = lax.broadcasted_iota(jnp.int32, (GROUP, 1), 0) // DA_QK_DIM
    for i in range(n_maps):
        qm_scr[i] = qt * (row_map == i).astype(BF16)

    def scores(off, width, i, slot):
        s = jnp.dot(k_ref[pl.ds(off, width), :], qm_scr[i], preferred_element_type=F32)
        s_scr[slot, 0:width, :] = s
        mx_scr[slot] = jnp.max(s, axis=0, keepdims=True)

    def accumulate(off, width, i, slot):
        m_prev = m_scr[i]
        m_next = jnp.maximum(m_prev, mx_scr[slot])
        p = jnp.exp2(s_scr[slot, 0:width, :] - m_next).astype(BF16)
        va = vta_ref[i // 2, :, pl.ds(off, width)]
        acc_scr[i] = jnp.exp2(m_prev - m_next) * acc_scr[i] + jnp.dot(va, p, preferred_element_type=F32)
        m_scr[i] = m_next

    is_ctx = qi * tq >= n_lat
    n_chunks = t // tk

    @pl.when(jnp.logical_not(is_ctx))
    def _():
        scores(0, tk, 0, 0)

        def body(c, carry):
            off = pl.multiple_of(c * tk, tk)
            off_next = pl.multiple_of(jnp.minimum(c + 1, n_chunks - 1) * tk, tk)
            for i in range(n_maps):
                if i + 1 < n_maps:
                    scores(off, tk, i + 1, (i + 1) % 2)
                else:
                    scores(off_next, tk, 0, 0)
                accumulate(off, tk, i, i % 2)
            return carry

        lax.fori_loop(0, n_chunks, body, 0)

    @pl.when(is_ctx)
    def _():
        for i in range(n_maps):
            scores(n_lat, t - n_lat, i, i % 2)
            accumulate(n_lat, t - n_lat, i, i % 2)

    lam = lam_ref[0]
    outs = []
    for h in range(DA_HEADS):
        a0, a1 = acc_scr[2 * h], acc_scr[2 * h + 1]
        o = (a0[0:DA_V_DIM] / a0[DA_V_DIM:DA_V_DIM + 1]
             - lam * (a1[0:DA_V_DIM] / a1[DA_V_DIM:DA_V_DIM + 1]))
        inv = lax.rsqrt(jnp.mean(o * o, axis=0, keepdims=True) + NORM_EPS)
        outs.append(o * inv * gain_ref[...] * fin_scale)
    o_ref[...] = jnp.concatenate(outs, axis=0).T.astype(BF16)


def _diff_attention(z, qt, vta, lam, gain, n_lat, fin_scale):
    b, t, _ = z.shape
    tq = ATT_BLOCK
    tk = max(w for w in range(ATT_BLOCK, DA_MAX_KEY_CHUNK + 1, ATT_BLOCK) if t % w == 0)
    n_maps = 2 * DA_HEADS
    va_rows = vta.shape[2]
    return pl.pallas_call(
        functools.partial(_diff_attn_kernel, n_lat=n_lat, t=t, tq=tq, tk=tk, fin_scale=fin_scale),
        grid=(b, t // tq),
        in_specs=[pl.BlockSpec(memory_space=pltpu.SMEM),
                  pl.BlockSpec((None, GROUP, tq), lambda bi, m: (bi, 0, m)),
                  pl.BlockSpec((None, t, GROUP), lambda bi, m: (bi, 0, 1)),
                  pl.BlockSpec((None, DA_HEADS, va_rows, t), lambda bi, m: (bi, 0, 0, 0)),
                  pl.BlockSpec((DA_V_DIM, tq), lambda bi, m: (0, 0))],
        out_specs=pl.BlockSpec((None, tq, GROUP), lambda bi, m: (bi, m, 0)),
        out_shape=jax.ShapeDtypeStruct((b, t, GROUP), BF16),
        scratch_shapes=[pltpu.VMEM((n_maps, GROUP, tq), BF16), pltpu.VMEM((2, tk, tq), F32),
                        pltpu.VMEM((2, 1, tq), F32), pltpu.VMEM((n_maps, 1, tq), F32),
                        pltpu.VMEM((n_maps, va_rows, tq), F32)],
        compiler_params=_cparams(2),
    )(lam, qt, z, vta, gain)


def _retention_kernel(q_ref, k_ref, v_ref, g_ref, dm_ref, qw_ref, kw_ref, cd_ref, bd_ref, o_ref,
                      st_scr, of_scr, *, c, n_chunks, n_lat_chunks):
    d = pl.program_id(1)
    i = pl.program_id(2)
    chunk = jnp.where(d == 0, (i + n_lat_chunks) % n_chunks, n_chunks - 1 - i)

    @pl.when(i == 0)
    def _():
        st_scr[...] = jnp.zeros(st_scr.shape, F32)

    q, k, v = q_ref[...], k_ref[...], v_ref[...]
    lane_head = lax.broadcasted_iota(jnp.int32, (1, GROUP), 1) // RET_DIM
    o = jnp.dot((q.astype(F32) * qw_ref[...]).astype(BF16), st_scr[...].astype(BF16),
                preferred_element_type=F32)
    for h in range(RET_HEADS):
        mh = lane_head == h
        qh = q * mh.astype(BF16)
        s = lax.dot_general(qh, k, (((1,), (1,)), ((), ())), preferred_element_type=F32) * dm_ref[h]
        o = o + jnp.where(mh, jnp.dot(s.astype(BF16), v, preferred_element_type=F32), 0.0)
    kwk = (k.astype(F32) * kw_ref[...]).astype(BF16)
    kv = lax.dot_general(kwk, v, (((0,), (0,)), ((), ())), preferred_element_type=F32)
    st_scr[...] = cd_ref[...] * st_scr[...] + bd_ref[...] * kv
    off = pl.multiple_of(chunk * c, c)

    @pl.when(d == 0)
    def _():
        of_scr[pl.ds(off, c), :] = o

    @pl.when(d == 1)
    def _():
        tot = o + of_scr[pl.ds(off, c), :]
        sq = tot * tot
        inv = jnp.zeros(tot.shape, F32)
        for h in range(RET_HEADS):
            mh = lane_head == h
            ms = jnp.sum(jnp.where(mh, sq, 0.0), axis=-1, keepdims=True) * (1.0 / RET_DIM)
            inv = jnp.where(mh, lax.rsqrt(ms + NORM_EPS), inv)
        o_ref[...] = (tot * inv * jax.nn.silu(g_ref[...].astype(F32))).astype(BF16)


def _retention_tables(decay_logit, c):
    log_g = jax.nn.log_sigmoid(decay_logit.astype(F32))
    pos = jnp.arange(c, dtype=F32)
    rel = pos[:, None] - pos[None, :]
    rel = jnp.stack([rel, -rel])
    dm = jnp.where(rel[:, None] >= 0, jnp.exp(log_g[:, :, None, None] * jnp.maximum(rel[:, None], 0.0)), 0.0)
    lg_lane = jnp.repeat(log_g, RET_DIM, axis=1)
    q_exp = jnp.stack([pos + 1.0, c - pos])
    k_exp = jnp.stack([c - 1.0 - pos, pos])
    qw = jnp.exp(lg_lane[:, None, :] * q_exp[:, :, None])
    kw = jnp.exp(lg_lane[:, None, :] * k_exp[:, :, None])
    head = np.arange(GROUP) // RET_DIM
    bd = jnp.asarray((head[:, None] == head[None, :]).astype(np.float32))
    cd = jnp.exp(lg_lane * c)[:, :, None] * jnp.ones((1, 1, GROUP), F32)
    return dm, qw, kw, cd, bd


def _retention(z, tables, layer, n_lat):
    b, t, _ = z.shape
    c = ATT_BLOCK
    n_chunks, n_lat_chunks = t // c, n_lat // c
    dm, qw, kw, cd, bd = tables

    def chunk_of(d, i):
        return jnp.where(d == 0, (i + n_lat_chunks) % n_chunks, n_chunks - 1 - i)

    def zspec(col):
        return pl.BlockSpec((None, c, GROUP), lambda bi, d, i: (bi, chunk_of(d, i), col))

    return pl.pallas_call(
        functools.partial(_retention_kernel, c=c, n_chunks=n_chunks, n_lat_chunks=n_lat_chunks),
        grid=(b, 2, n_chunks),
        in_specs=[zspec(3), zspec(4), zspec(5), zspec(6),
                  pl.BlockSpec((None, None, RET_HEADS, c, c), lambda bi, d, i: (layer, d, 0, 0, 0)),
                  pl.BlockSpec((None, None, c, GROUP), lambda bi, d, i: (layer, d, 0, 0)),
                  pl.BlockSpec((None, None, c, GROUP), lambda bi, d, i: (layer, d, 0, 0)),
                  pl.BlockSpec((None, None, GROUP, GROUP), lambda bi, d, i: (layer, d, 0, 0)),
                  pl.BlockSpec((None, GROUP, GROUP), lambda bi, d, i: (layer, 0, 0))],
        out_specs=pl.BlockSpec((None, c, GROUP),
                               lambda bi, d, i: (bi, jnp.where(d == 0, n_chunks - 1, n_chunks - 1 - i), 0)),
        out_shape=jax.ShapeDtypeStruct((b, t, GROUP), BF16),
        scratch_shapes=[pltpu.VMEM((GROUP, GROUP), F32), pltpu.VMEM((t, GROUP), F32)],
        compiler_params=_cparams(3),
    )(z, z, z, z, dm, qw, kw, cd, bd)


def _nbr_attn_kernel(q_ref, kp_ref, kc_ref, kn_ref, kx_ref, vp_ref, vc_ref, vn_ref, vx_ref, tab_ref, o_ref):
    q = q_ref[...]
    k = jnp.concatenate([kp_ref[...], kc_ref[...], kn_ref[...], kx_ref[...]], axis=0)
    v = jnp.concatenate([vp_ref[...], vc_ref[...], vn_ref[...], vx_ref[...]], axis=0)
    lane_head = lax.broadcasted_iota(jnp.int32, (1, GROUP), 1) // NA_DIM
    o = jnp.zeros(q.shape, F32)
    for h in range(NA_HEADS):
        mh = lane_head == h
        qh = q * mh.astype(BF16)
        s = lax.dot_general(qh, k, (((1,), (1,)), ((), ())), preferred_element_type=F32) + tab_ref[h]
        p = jnp.exp(s - jnp.max(s, axis=-1, keepdims=True))
        l = jnp.sum(p, axis=-1, keepdims=True)
        o = o + jnp.where(mh, jnp.dot(p.astype(BF16), v, preferred_element_type=F32) / l, 0.0)
    o_ref[...] = o.astype(BF16)


def _nbr_tables(rpb, n_lat, n_ctx):
    blk_rows = ATT_BLOCK // GRID_W
    rows = n_lat // GRID_W
    n_blk = rows // blk_rows
    assert rows >= NA_WIN_ROWS and n_blk >= 3
    qcol = kcol = np.arange(GRID_W)
    c0 = np.clip(qcol - NA_WIN_COLS // 2, 0, GRID_W - NA_WIN_COLS)
    col_ok = (kcol[None, :] >= c0[:, None]) & (kcol[None, :] < c0[:, None] + NA_WIN_COLS)
    dc = np.clip(kcol[None, :] - qcol[:, None] + (NA_WIN_COLS - 1), 0, 2 * NA_WIN_COLS - 2)
    oh_c = np.eye(2 * NA_WIN_COLS - 1, dtype=np.float32)[dc]
    qr, kr = np.arange(blk_rows), np.arange(3 * blk_rows)
    oh_r, oks = [], []
    for m in (0, 1, n_blk - 1):
        qrow, krow = blk_rows * m + qr, blk_rows * (m - 1) + kr
        r0 = np.clip(qrow - NA_WIN_ROWS // 2, 0, rows - NA_WIN_ROWS)
        row_ok = (krow[None, :] >= r0[:, None]) & (krow[None, :] < r0[:, None] + NA_WIN_ROWS)
        dr = np.clip(krow[None, :] - qrow[:, None] + (NA_WIN_ROWS - 1), 0, 2 * NA_WIN_ROWS - 2)
        oh_r.append(np.eye(2 * NA_WIN_ROWS - 1, dtype=np.float32)[dr])
        oks.append((row_ok[:, None, :, None] & col_ok[None, :, None, :]).reshape(ATT_BLOCK, 3 * ATT_BLOCK))
    bias = jnp.einsum('vqka,hab,QKb->vhqQkK', jnp.asarray(np.stack(oh_r)), rpb.astype(F32), jnp.asarray(oh_c),
                      precision=lax.Precision.HIGHEST).reshape(3, NA_HEADS, ATT_BLOCK, 3 * ATT_BLOCK)
    tab = jnp.where(jnp.asarray(np.stack(oks))[:, None], bias, NEG_BIG)
    tab = jnp.concatenate([tab, jnp.full((1, NA_HEADS, ATT_BLOCK, 3 * ATT_BLOCK), NEG_BIG, F32)], axis=0)
    return jnp.concatenate([tab, jnp.zeros((4, NA_HEADS, ATT_BLOCK, n_ctx), F32)], axis=-1)


def _nbr_attention(z, tab, layer, n_lat):
    b, t, _ = z.shape
    n_ctx = t - n_lat
    assert n_ctx == ATT_BLOCK and n_lat % ATT_BLOCK == 0
    n_blk = n_lat // ATT_BLOCK

    def kv_specs(col):
        mk = lambda f: pl.BlockSpec((None, ATT_BLOCK, GROUP), lambda bi, m: (bi, f(m), col))
        return [mk(lambda m: jnp.clip(m - 1, 0, n_blk - 1)), mk(lambda m: jnp.minimum(m, n_blk - 1)),
                mk(lambda m: jnp.clip(m + 1, 0, n_blk - 1)), mk(lambda m: n_blk)]

    def variant(m):
        return jnp.where(m == 0, 0, jnp.where(m == n_blk - 1, 2, jnp.where(m == n_blk, 3, 1)))

    win = 3 * ATT_BLOCK + n_ctx
    return pl.pallas_call(
        _nbr_attn_kernel,
        grid=(b, n_blk + 1),
        in_specs=[pl.BlockSpec((None, ATT_BLOCK, GROUP), lambda bi, m: (bi, m, 7))] + kv_specs(8) + kv_specs(9)
                 + [pl.BlockSpec((None, None, NA_HEADS, ATT_BLOCK, win),
                                 lambda bi, m: (layer, variant(m), 0, 0, 0))],
        out_specs=pl.BlockSpec((None, ATT_BLOCK, GROUP), lambda bi, m: (bi, m, 0)),
        out_shape=jax.ShapeDtypeStruct((b, t, GROUP), BF16),
        compiler_params=_cparams(2),
    )(*([z] * 9), tab)


def _s5_kernel(u_ref, x0_ref, kb_ref, sm_ref, rm_ref, ap_ref, y_ref, xf_ref, st_scr, sloc_scr, xin_scr,
               *, rb):
    d = pl.program_id(0)
    i = pl.program_id(2)

    @pl.when(i == 0)
    def _():
        st_scr[...] = x0_ref[...]

    u = u_ref[...]
    sloc_scr[...] = jnp.dot(u, sm_ref[...], preferred_element_type=F32)
    ar, ai = ap_ref[0:1, :], ap_ref[1:2, :]
    ns = S5_NSTATE

    def step(n, carry):
        xr, xi = carry
        r = jnp.where(d == 0, n, rb - 1 - n)
        xin_scr[pl.ds(r, 1), 0:ns] = xr
        xin_scr[pl.ds(r, 1), ns:2 * ns] = xi
        s = sloc_scr[pl.ds(r, 1), :]
        return ar * xr - ai * xi + s[:, 0:ns], ar * xi + ai * xr + s[:, ns:2 * ns]

    xr, xi = lax.fori_loop(0, rb, step, (st_scr[:, 0:ns], st_scr[:, ns:2 * ns]))
    st_scr[:, 0:ns] = xr
    st_scr[:, ns:2 * ns] = xi
    y_state = jnp.dot(xin_scr[...].astype(BF16), rm_ref[...], preferred_element_type=F32)
    for direction in range(2):
        @pl.when(d == direction)
        def _():
            for t_out in range(S5_CHUNK):
                srcs = range(0, t_out + 1) if direction == 0 else range(t_out, S5_CHUNK)
                acc = y_state[:, t_out * GROUP:(t_out + 1) * GROUP]
                for t_in in srcs:
                    acc = acc + jnp.dot(u[:, t_in * GROUP:(t_in + 1) * GROUP], kb_ref[abs(t_out - t_in)],
                                        preferred_element_type=F32)
                y_ref[:, t_out * GROUP:(t_out + 1) * GROUP] = acc
    xf_ref[...] = st_scr[...]


def _s5_operators(a_re, a_im, b_re, b_im, c_re, c_im, log_step):
    tc, g, p, ch = S5_CHUNK, S5_GROUPS, S5_STATE, S5_CH
    hi = lax.Precision.HIGHEST
    ar, ai = a_re.astype(F32), a_im.astype(F32)
    dt = jnp.exp(log_step.astype(F32))[..., None]
    steps = jnp.arange(tc + 1, dtype=F32)[None, :, None, None]
    mag = jnp.exp((ar * dt)[:, None] * steps)
    ang = (ai * dt)[:, None] * steps
    pw_r, pw_i = mag * jnp.cos(ang), mag * jnp.sin(ang)
    x, y, den = pw_r[:, 1] - 1.0, pw_i[:, 1], ar * ar + ai * ai
    cf_r, cf_i = ((x * ar + y * ai) / den)[..., None], ((y * ar - x * ai) / den)[..., None]
    bre, bim = b_re.astype(F32), b_im.astype(F32)
    bb_r, bb_i = cf_r * bre - cf_i * bim, cf_r * bim + cf_i * bre
    cr, ci = c_re.astype(F32), c_im.astype(F32)
    w_r = pw_r[..., None] * bb_r[:, None] - pw_i[..., None] * bb_i[:, None]
    w_i = pw_r[..., None] * bb_i[:, None] + pw_i[..., None] * bb_r[:, None]
    kern = (jnp.einsum('dgcp,dtgph->dtghc', cr, w_r, precision=hi)
            - jnp.einsum('dgcp,dtgph->dtghc', ci, w_i, precision=hi))
    eye = jnp.eye(g, dtype=BF16)
    idx = np.arange(tc)
    kb = (kern[:, 0:tc, :, :, None, :].astype(BF16)
          * eye[None, None, :, None, :, None]).reshape(2, tc, g * ch, g * ch)
    outs = []
    for d in range(2):
        f_exp = jnp.asarray((tc - 1 - idx) if d == 0 else idx)
        sm = jnp.concatenate(
            [(jnp.transpose(part[d][f_exp], (0, 1, 3, 2)).astype(BF16)[:, :, :, None, :]
              * eye[None, :, None, :, None]).reshape(tc * g * ch, g * p) for part in (w_r, w_i)], axis=1)
        e_exp = jnp.asarray((idx + 1) if d == 0 else (tc - idx))
        pr, pi_ = pw_r[d][e_exp][:, :, None, :], pw_i[d][e_exp][:, :, None, :]
        ca_r, ca_i = cr[d][None] * pr - ci[d][None] * pi_, cr[d][None] * pi_ + ci[d][None] * pr
        rm = jnp.concatenate(
            [(jnp.transpose(part, (1, 3, 0, 2)).astype(BF16)[:, :, :, None, :]
              * eye[:, None, None, :, None]).reshape(g * p, tc * g * ch) for part in (ca_r, -ca_i)], axis=0)
        ap = jnp.stack([pw_r[d, tc].reshape(-1), pw_i[d, tc].reshape(-1)])
        outs.append((sm, rm, ap))
    sm, rm, ap = (jnp.stack(v) for v in zip(*outs))
    return kb, sm, rm, ap


def _s5_scan(u_rows, row0, r, x0, ops, layer, rb):
    b = u_rows.shape[0]
    nb = r // rb
    assert r % rb == 0 and row0 % rb == 0
    kb, sm, rm, ap = ops
    blk = lambda d, i: jnp.where(d == 0, i, nb - 1 - i)
    op_spec = lambda shp: pl.BlockSpec((None, None) + shp, lambda d, bi, i: (layer, d) + (0,) * len(shp),
                                       pipeline_mode=pl.Buffered(1))
    return pl.pallas_call(
        functools.partial(_s5_kernel, rb=rb),
        grid=(2, b, nb),
        in_specs=[pl.BlockSpec((None, rb, S5_ROW), lambda d, bi, i: (bi, row0 // rb + blk(d, i), 0)),
                  pl.BlockSpec((None, None, 1, 2 * S5_NSTATE), lambda d, bi, i: (d, bi, 0, 0)),
                  op_spec((S5_CHUNK, GROUP, GROUP)), op_spec((S5_ROW, 2 * S5_NSTATE)),
                  op_spec((2 * S5_NSTATE, S5_ROW)),
                  pl.BlockSpec((None, None, 2, S5_NSTATE), lambda d, bi, i: (layer, d, 0, 0))],
        out_specs=[pl.BlockSpec((None, None, rb, S5_ROW), lambda d, bi, i: (d, bi, blk(d, i), 0)),
                   pl.BlockSpec((None, None, 1, 2 * S5_NSTATE), lambda d, bi, i: (d, bi, 0, 0))],
        out_shape=[jax.ShapeDtypeStruct((2, b, r, S5_ROW), F32),
                   jax.ShapeDtypeStruct((2, b, 1, 2 * S5_NSTATE), F32)],
        scratch_shapes=[pltpu.VMEM((1, 2 * S5_NSTATE), F32), pltpu.VMEM((rb, 2 * S5_NSTATE), F32),
                        pltpu.VMEM((rb, 2 * S5_NSTATE), F32)],
        compiler_params=_cparams(3),
    )(u_rows, x0, kb, sm, rm, ap)


def _s5_glu_kernel(y_ref, u_ref, d_ref, w_ref, b_ref, o_ref, y_scr, *, rb):
    yf = y_ref[0] + y_ref[1]
    for r in range(S5_CHUNK):
        for hf in range(GROUP // 128):
            lo = r * GROUP + hf * 128
            y_scr[hf, pl.ds(r, rb, stride=S5_CHUNK), :] = yf[:, lo:lo + 128]
    y = (u_ref[...].astype(F32) * d_ref[...]
         + jnp.concatenate([y_scr[hf] for hf in range(GROUP // 128)], axis=-1))
    act = jax.nn.gelu(y)
    gate = jnp.dot(act.astype(BF16), w_ref[...], preferred_element_type=F32) + b_ref[...]
    o_ref[...] = (act * jax.nn.sigmoid(gate)).astype(BF16)


def _s5_glu(y, z, tok0, rb, d_skip, glu_w, glu_b):
    b, r = y.shape[1], y.shape[2]
    tm = rb * S5_CHUNK
    assert r % rb == 0 and tok0 % tm == 0
    return pl.pallas_call(
        functools.partial(_s5_glu_kernel, rb=rb),
        grid=(b, r // rb),
        in_specs=[pl.BlockSpec((2, None, rb, S5_ROW), lambda bi, m: (0, bi, m, 0)),
                  pl.BlockSpec((None, tm, GROUP), lambda bi, m: (bi, tok0 // tm + m, 10)),
                  pl.BlockSpec((1, GROUP), lambda bi, m: (0, 0)),
                  pl.BlockSpec((GROUP, GROUP), lambda bi, m: (0, 0)),
                  pl.BlockSpec((1, GROUP), lambda bi, m: (0, 0))],
        out_specs=pl.BlockSpec((None, tm, GROUP), lambda bi, m: (bi, m, 0)),
        out_shape=jax.ShapeDtypeStruct((b, r * S5_CHUNK, GROUP), BF16),
        scratch_shapes=[pltpu.VMEM((GROUP // 128, tm, 128), F32)],
        compiler_params=_cparams(2),
    )(y, z, d_skip, glu_w, glu_b)


def _s5_mixer(z, u_rows, ops, layer, d_skip, glu_w, glu_b, n_lat):
    b, t, _ = z.shape
    r_lat, r_ctx = n_lat // S5_CHUNK, (t - n_lat) // S5_CHUNK
    zero = jnp.zeros((2, b, 1, 2 * S5_NSTATE), F32)
    y_ctx, x_ctx = _s5_scan(u_rows, r_lat, r_ctx, zero, ops, layer, r_ctx)
    y_lat, _ = _s5_scan(u_rows, 0, r_lat, x_ctx, ops, layer, _pick_tile(r_lat, 256))
    glu = (d_skip.reshape(1, GROUP).astype(F32), glu_w.astype(BF16), glu_b.reshape(1, GROUP).astype(F32))
    s_lat = _s5_glu(y_lat, z, 0, _pick_tile(r_lat, 128), *glu)
    s_ctx = _s5_glu(y_ctx, z, n_lat, r_ctx, *glu)
    return jnp.concatenate([s_lat, s_ctx], axis=1)


def _out_proj_kernel(a_ref, r_ref, n_ref, s_ref, w_ref, x_ref, mod_ref, gain_ref, o_ref, *, n_lat, tm, d):
    m = pl.program_id(1)
    mix = jnp.dot(a_ref[...], w_ref[0:GROUP, :], preferred_element_type=F32)
    for j, ref in enumerate((r_ref, n_ref, s_ref), start=1):
        mix = mix + jnp.dot(ref[...], w_ref[j * GROUP:(j + 1) * GROUP, :], preferred_element_type=F32)
    gate = _row_mod(mod_ref, 2, m, tm, n_lat, d)
    o_ref[...] = x_ref[...] + gate * (_rms_rows(mix) * gain_ref[...])


def _out_proj(parts, w_bf, x, mod, gain, n_lat):
    b, t, d = x.shape
    tm = _pick_tile(t, 528)
    part_spec = pl.BlockSpec((None, tm, GROUP), lambda bi, m: (bi, m, 0))
    row_spec = pl.BlockSpec((None, tm, d), lambda bi, m: (bi, m, 0))
    return pl.pallas_call(
        functools.partial(_out_proj_kernel, n_lat=n_lat, tm=tm, d=d),
        grid=(b, t // tm),
        in_specs=[part_spec] * 4 + [pl.BlockSpec((d, d), lambda bi, m: (0, 0)), row_spec,
                                    pl.BlockSpec((None, 2, mod.shape[-1]), lambda bi, m: (bi, 0, 0)),
                                    pl.BlockSpec((1, d), lambda bi, m: (0, 0))],
        out_specs=row_spec,
        out_shape=jax.ShapeDtypeStruct((b, t, d), F32),
        compiler_params=_cparams(2),
    )(*parts, w_bf, x, mod, gain)


def _moe_norm_kernel(x_ref, mod_ref, gain_ref, router_ref, h_ref, route_ref, *, n_lat, tm, d):
    m = pl.program_id(1)
    shift = _row_mod(mod_ref, 3, m, tm, n_lat, d)
    scale = _row_mod(mod_ref, 4, m, tm, n_lat, d)
    h = (_rms_rows(x_ref[...]) * gain_ref[...]) * (1.0 + scale) + shift
    h_ref[...] = h.astype(BF16)
    logits = jnp.dot(h, router_ref[...], precision=lax.Precision.HIGHEST, preferred_element_type=F32)
    lane = lax.broadcasted_iota(jnp.int32, logits.shape, 1)
    logits = jnp.where(lane < MOE_EXPERTS, logits, -jnp.inf)
    v1 = jnp.max(logits, axis=-1, keepdims=True)
    i1 = jnp.min(jnp.where(logits == v1, lane, 128), axis=-1, keepdims=True)
    rest_l = jnp.where(lane == i1, -jnp.inf, logits)
    v2 = jnp.max(rest_l, axis=-1, keepdims=True)
    i2 = jnp.min(jnp.where(rest_l == v2, lane, 128), axis=-1, keepdims=True)
    e = jnp.exp(v2 - v1)
    w1 = 1.0 / (1.0 + e)
    w2 = e / (1.0 + e)
    route = jnp.where(lane == 0, w1, jnp.where(lane == 1, w2, jnp.where(
        lane == 2, i1.astype(F32), jnp.where(lane == 3, i2.astype(F32), 0.0))))
    route_ref[...] = route


def _moe_norm(x, mod, gain, router, n_lat):
    b, t, d = x.shape
    tm = _pick_tile(t, 528)
    row_spec = pl.BlockSpec((None, tm, d), lambda bi, m: (bi, m, 0))
    return pl.pallas_call(
        functools.partial(_moe_norm_kernel, n_lat=n_lat, tm=tm, d=d),
        grid=(b, t // tm),
        in_specs=[row_spec, pl.BlockSpec((None, 2, mod.shape[-1]), lambda bi, m: (bi, 0, 0)),
                  pl.BlockSpec((1, d), lambda bi, m: (0, 0)), pl.BlockSpec((d, 128), lambda bi, m: (0, 0))],
        out_specs=[row_spec, pl.BlockSpec((None, tm, 128), lambda bi, m: (bi, m, 0))],
        out_shape=[jax.ShapeDtypeStruct((b, t, d), BF16), jax.ShapeDtypeStruct((b, t, 128), F32)],
        compiler_params=_cparams(2),
    )(x, mod, gain, jnp.pad(router.astype(F32), ((0, 0), (0, 128 - MOE_EXPERTS))))


def _swiglu_kernel(te_ref, nu_ref, x_ref, wg_ref, wu_ref, wd_ref, o_ref, acc_scr):
    tile = pl.program_id(0)
    f = pl.program_id(1)
    nf = pl.num_programs(1)
    used = tile < nu_ref[0]

    @pl.when(f == 0)
    def _():
        acc_scr[...] = jnp.zeros(acc_scr.shape, F32)

    @pl.when(used)
    def _():
        x = x_ref[...]
        g = jnp.dot(x, wg_ref[...], preferred_element_type=F32)
        u = jnp.dot(x, wu_ref[...], preferred_element_type=F32)
        acc_scr[...] += jnp.dot((jax.nn.silu(g) * u).astype(BF16), wd_ref[...], preferred_element_type=F32)

    @pl.when(f == nf - 1)
    def _():
        o_ref[...] = acc_scr[...].astype(BF16)


def _swiglu_grouped(xs, tile_expert, n_used, w_gate, w_up, w_down, tm):
    p, d = xs.shape
    n_exp, _, f = w_gate.shape
    tf = f // 2
    grid_spec = pltpu.PrefetchScalarGridSpec(
        num_scalar_prefetch=2,
        grid=(p // tm, f // tf),
        in_specs=[pl.BlockSpec((tm, d), lambda t, j, te, nu: (t, 0)),
                  pl.BlockSpec((None, d, tf), lambda t, j, te, nu: (te[t], 0, j)),
                  pl.BlockSpec((None, d, tf), lambda t, j, te, nu: (te[t], 0, j)),
                  pl.BlockSpec((None, tf, d), lambda t, j, te, nu: (te[t], j, 0))],
        out_specs=pl.BlockSpec((tm, d), lambda t, j, te, nu: (t, 0)),
        scratch_shapes=[pltpu.VMEM((tm, d), F32)])
    return pl.pallas_call(
        _swiglu_kernel, grid_spec=grid_spec,
        out_shape=jax.ShapeDtypeStruct((p, d), BF16),
        compiler_params=_cparams(2),
    )(tile_expert, n_used, xs, w_gate, w_up, w_down)


def _moe_residual_kernel(y1_ref, y2_ref, route_ref, x_ref, mod_ref, gain_ref, o_ref, *, n_lat, tm, d):
    m = pl.program_id(1)
    y = route_ref[:, 0:1] * y1_ref[...].astype(F32) + route_ref[:, 1:2] * y2_ref[...].astype(F32)
    gate = _row_mod(mod_ref, 5, m, tm, n_lat, d)
    o_ref[...] = x_ref[...] + gate * (_rms_rows(y) * gain_ref[...])


def _moe_residual(y12, route, x, mod, gain, n_lat, rows):
    b, t, d = x.shape
    tm = _pick_tile(rows, 528)
    row_spec = pl.BlockSpec((None, tm, d), lambda bi, m: (bi, m, 0))
    return pl.pallas_call(
        functools.partial(_moe_residual_kernel, n_lat=n_lat, tm=tm, d=d),
        grid=(b, rows // tm),
        in_specs=[pl.BlockSpec((None, None, tm, d), lambda bi, m: (0, bi, m, 0)),
                  pl.BlockSpec((None, None, tm, d), lambda bi, m: (1, bi, m, 0)),
                  pl.BlockSpec((None, tm, 128), lambda bi, m: (bi, m, 0)), row_spec,
                  pl.BlockSpec((None, 2, mod.shape[-1]), lambda bi, m: (bi, 0, 0)),
                  pl.BlockSpec((1, d), lambda bi, m: (0, 0))],
        out_specs=row_spec,
        out_shape=jax.ShapeDtypeStruct((b, rows, d), F32),
        compiler_params=_cparams(2),
    )(y12, y12, route, x, mod, gain)


def _dense_ffn_kernel(x_ref, mod_ref, gin_ref, gout_ref, wg_ref, wu_ref, wd_ref, o_ref, h_scr, acc_scr,
                      *, n_lat, tm, d):
    m = pl.program_id(1)
    f = pl.program_id(2)

    @pl.when(f == 0)
    def _():
        shift = _row_mod(mod_ref, 3, m, tm, n_lat, d)
        scale = _row_mod(mod_ref, 4, m, tm, n_lat, d)
        h_scr[...] = ((_rms_rows(x_ref[...]) * gin_ref[...]) * (1.0 + scale) + shift).astype(BF16)
        acc_scr[...] = jnp.zeros(acc_scr.shape, F32)

    h = h_scr[...]
    g = jnp.dot(h, wg_ref[...], preferred_element_type=F32)
    u = jnp.dot(h, wu_ref[...], preferred_element_type=F32)
    acc_scr[...] += jnp.dot((jax.nn.silu(g) * u).astype(BF16), wd_ref[...], preferred_element_type=F32)

    @pl.when(f == pl.num_programs(2) - 1)
    def _():
        gate = _row_mod(mod_ref, 5, m, tm, n_lat, d)
        o_ref[...] = x_ref[...] + gate * (_rms_rows(acc_scr[...]) * gout_ref[...])


def _dense_ffn(x, mod, gain_in, gain_out, w_gate, w_up, w_down, n_lat):
    b, t, d = x.shape
    f = w_gate.shape[1]
    tm = _pick_tile(t, 528)
    tf = f // 2
    row_spec = pl.BlockSpec((None, tm, d), lambda bi, m, j: (bi, m, 0))
    vec_spec = pl.BlockSpec((1, d), lambda bi, m, j: (0, 0))
    return pl.pallas_call(
        functools.partial(_dense_ffn_kernel, n_lat=n_lat, tm=tm, d=d),
        grid=(b, t // tm, f // tf),
        in_specs=[row_spec, pl.BlockSpec((None, 2, mod.shape[-1]), lambda bi, m, j: (bi, 0, 0)),
                  vec_spec, vec_spec,
                  pl.BlockSpec((d, tf), lambda bi, m, j: (0, j)), pl.BlockSpec((d, tf), lambda bi, m, j: (0, j)),
                  pl.BlockSpec((tf, d), lambda bi, m, j: (j, 0))],
        out_specs=row_spec,
        out_shape=jax.ShapeDtypeStruct((b, t, d), F32),
        scratch_shapes=[pltpu.VMEM((tm, d), BF16), pltpu.VMEM((tm, d), F32)],
        compiler_params=_cparams(3),
    )(x, mod, gain_in, gain_out, w_gate, w_up, w_down)


def _moe_ffn(h, route, w_gate, w_up, w_down):
    b, t, d = h.shape
    n = b * t
    tm = _pick_tile(2 * n, 512)
    route = route.reshape(n, 128)
    expert = jnp.concatenate([route[:, 2], route[:, 3]]).astype(jnp.int32)
    token = jnp.concatenate([jnp.arange(n, dtype=jnp.int32)] * 2)
    onehot = (expert[:, None] == jnp.arange(MOE_EXPERTS, dtype=jnp.int32)[None, :]).astype(jnp.int32)
    counts = jnp.sum(onehot, axis=0)
    rank = jnp.sum((jnp.cumsum(onehot, axis=0) - onehot) * onehot, axis=1)
    padded = ((counts + tm - 1) // tm) * tm
    ends = jnp.cumsum(padded)
    pos = jnp.sum(onehot * (ends - padded)[None, :], axis=1) + rank
    n_rows = 2 * n + MOE_EXPERTS * tm
    n_tiles = n_rows // tm
    tile_start = jnp.arange(n_tiles, dtype=jnp.int32) * tm
    tile_expert = jnp.minimum(jnp.sum((ends[None, :] <= tile_start[:, None]).astype(jnp.int32), axis=1),
                              MOE_EXPERTS - 1)
    n_used = (ends[-1] // tm).astype(jnp.int32).reshape(1)
    row_token = jnp.zeros((n_rows,), jnp.int32).at[pos].set(token)
    xs = jnp.take(h.reshape(n, d), row_token, axis=0)
    out = _swiglu_grouped(xs, tile_expert, n_used, w_gate, w_up, w_down, tm)
    return jnp.take(out, pos, axis=0).reshape(2, b, t, d)


def kernel(x, c, ctx, c_ctx, w_mod, b_mod, norm_gain, w_in, w_out, da_lambda, da_norm_gain,
           ret_decay_logit, na_rpb, s5_a_re, s5_a_im, s5_b_re, s5_b_im, s5_c_re, s5_c_im,
           s5_log_step, s5_d, s5_glu_w, s5_glu_b, ffn_w_gate, ffn_w_up, ffn_w_down,
           moe_router, moe_w_gate, moe_w_up, moe_w_down):
    b, n_lat, d = x.shape
    n_ctx = ctx.shape[1]
    t = n_lat + n_ctx
    depth = w_mod.shape[0]
    assert d == 4 * GROUP and b + 1 <= 8

    c_rows = jnp.zeros((8, d), F32).at[:b].set(c).at[b].set(c_ctx)
    mod_all = _modulation(c_rows, w_mod, b_mod)
    tabs = _rope_tables(n_lat, t, DA_QK_DIM) + _rope_tables(n_lat, t, RET_DIM)
    h_all = jnp.concatenate([x, ctx], axis=1)

    lam_inits = [0.8 - 0.6 * math.exp(-0.3 * layer) for layer in range(depth)]
    lp = da_lambda.astype(F32)
    lams = (jnp.exp(jnp.sum(lp[:, 0] * lp[:, 1], axis=-1)) - jnp.exp(jnp.sum(lp[:, 2] * lp[:, 3], axis=-1))
            + jnp.asarray(lam_inits, F32))
    ret_tabs = jax.vmap(lambda dl: _retention_tables(dl, ATT_BLOCK))(ret_decay_logit)
    na_tabs = jax.vmap(lambda r: _nbr_tables(r, n_lat, n_ctx))(na_rpb)
    s5_ops = jax.vmap(_s5_operators)(s5_a_re, s5_a_im, s5_b_re, s5_b_im, s5_c_re, s5_c_im, s5_log_step)

    for layer in range(depth):
        mod = jnp.stack([mod_all[layer, :b], jnp.broadcast_to(mod_all[layer, b], (b, 6 * d))], axis=1)
        g = norm_gain[layer].astype(F32)
        lam_init = lam_inits[layer]

        z, qt, vta, u_rows = _in_proj(h_all, mod, g[0:1], w_in[layer].astype(BF16), tabs, n_lat)
        da_gain = jnp.broadcast_to(da_norm_gain[layer].astype(F32)[:, None], (DA_V_DIM, ATT_BLOCK))
        a_out = _diff_attention(z, qt, vta, lams[layer].reshape(1), da_gain, n_lat, 1.0 - lam_init)
        r_out = _retention(z, ret_tabs, layer, n_lat)
        n_out = _nbr_attention(z, na_tabs, layer, n_lat)
        s_out = _s5_mixer(z, u_rows, s5_ops, layer, s5_d[layer], s5_glu_w[layer], s5_glu_b[layer], n_lat)
        h_all = _out_proj((a_out, r_out, n_out, s_out), w_out[layer].astype(BF16), h_all, mod, g[1:2], n_lat)

        i = layer // 2
        if layer % 2 == 0:
            h_all = _dense_ffn(h_all, mod, g[2:3], g[3:4], ffn_w_gate[i].astype(BF16), ffn_w_up[i].astype(BF16),
                               ffn_w_down[i].astype(BF16), n_lat)
        else:
            f_in, route = _moe_norm(h_all, mod, g[2:3], moe_router[i], n_lat)
            y12 = _moe_ffn(f_in, route, moe_w_gate[i].astype(BF16), moe_w_up[i].astype(BF16),
                           moe_w_down[i].astype(BF16))
            h_all = _moe_residual(y12, route, h_all, mod, g[3:4], n_lat, n_lat if layer == depth - 1 else t)
    return h_all[:, :n_lat]
```

```python
import functools
import math

import numpy as np
import jax
import jax.numpy as jnp
from jax import lax
from jax.experimental import pallas as pl
from jax.experimental.pallas import tpu as pltpu

F32 = jnp.float32
BF16 = jnp.bfloat16

GRID_W = 64
GROUP = 256
N_IN_SLICES = 11
DA_HEADS, DA_QK_DIM, DA_V_DIM = 4, 32, 64
DA_ONES_ROWS = 16
DA_MAX_KEY_CHUNK = 4224
RET_HEADS, RET_DIM = 4, 64
NA_HEADS, NA_DIM, NA_WIN_ROWS, NA_WIN_COLS = 4, 64, 8, 16
S5_CH, S5_GROUPS, S5_STATE = 16, 16, 64
MOE_EXPERTS, MOE_TOP_K = 8, 2
ROPE_BASE = 10000.0
NORM_EPS = 1e-6
NEG_BIG = -1e30

S5_CHUNK = 8
S5_ROW = S5_CHUNK * GROUP
S5_NSTATE = S5_GROUPS * S5_STATE
ATT_BLOCK = 256
VMEM_LIMIT = 48 * 1024 * 1024


def _cparams(n_axes):
    return pltpu.CompilerParams(dimension_semantics=("arbitrary",) * n_axes,
                                vmem_limit_bytes=VMEM_LIMIT)


def _pick_tile(n, target, mult=16):
    best = None
    for t in range(mult, min(n, target) + 1, mult):
        if n % t == 0:
            best = t
    assert best is not None, (n, target)
    return best


def _rms_rows(x):
    return x * lax.rsqrt(jnp.mean(x * x, axis=-1, keepdims=True) + NORM_EPS)


def _row_mod(mod_ref, k, m, tm, n_lat, d):
    row = m * tm + lax.broadcasted_iota(jnp.int32, (tm, 1), 0)
    return jnp.where(row >= n_lat, mod_ref[1:2, k * d:(k + 1) * d], mod_ref[0:1, k * d:(k + 1) * d])


def _mod_kernel(c_ref, w_ref, b_ref, o_ref):
    o_ref[...] = jnp.dot(jax.nn.silu(c_ref[...]), w_ref[...], precision=lax.Precision.HIGHEST,
                         preferred_element_type=F32) + b_ref[...]


def _modulation(c_rows, w_mod, b_mod):
    depth, d, n6 = w_mod.shape
    tn = 1024
    return pl.pallas_call(
        _mod_kernel,
        grid=(depth, n6 // tn),
        in_specs=[pl.BlockSpec((8, d), lambda l, n: (0, 0)),
                  pl.BlockSpec((None, d, tn), lambda l, n: (l, 0, n)),
                  pl.BlockSpec((None, 1, tn), lambda l, n: (l, 0, n))],
        out_specs=pl.BlockSpec((None, 8, tn), lambda l, n: (l, 0, n)),
        out_shape=jax.ShapeDtypeStruct((depth, 8, n6), F32),
        compiler_params=_cparams(2),
    )(c_rows, w_mod, b_mod.reshape(depth, 1, n6))


def _rope(z, cos, sin_signed, q):
    n = z.shape[-1]
    lane = lax.broadcasted_iota(jnp.int32, z.shape, 1)
    first = (lane % (2 * q)) < q
    partner = jnp.where(first, pltpu.roll(z, n - q, 1), pltpu.roll(z, q, 1))
    return z * cos + partner * sin_signed


def _in_proj_kernel(x_ref, mod_ref, gain_ref, w_ref, cda_ref, sda_ref, crt_ref, srt_ref,
                    z_ref, qt_ref, vta_ref, uf_ref, u_scr, *, n_lat, tm, d):
    m = pl.program_id(1)
    shift = _row_mod(mod_ref, 0, m, tm, n_lat, d)
    scale = _row_mod(mod_ref, 1, m, tm, n_lat, d)
    h = ((_rms_rows(x_ref[...]) * gain_ref[...]) * (1.0 + scale) + shift).astype(BF16)
    col_scale = {0: DA_QK_DIM ** -0.5 * math.log2(math.e), 4: RET_DIM ** -0.5, 7: NA_DIM ** -0.5}
    for j in range(N_IN_SLICES):
        zj = jnp.dot(h, w_ref[:, j * GROUP:(j + 1) * GROUP], preferred_element_type=F32)
        if j in (0, 1):
            zj = _rope(zj, cda_ref[...], sda_ref[...], DA_QK_DIM // 4)
        elif j in (3, 4):
            zj = _rope(zj, crt_ref[...], srt_ref[...], RET_DIM // 4)
        if j in col_scale:
            zj = zj * col_scale[j]
        z_ref[:, j * GROUP:(j + 1) * GROUP] = zj.astype(BF16)
        if j == 0:
            qt_ref[...] = zj.T.astype(BF16)
        elif j == 2:
            vta_ref[:, 0:DA_V_DIM, :] = zj.T.reshape(DA_HEADS, DA_V_DIM, tm).astype(BF16)
            vta_ref[:, DA_V_DIM:, :] = jnp.ones((DA_HEADS, DA_ONES_ROWS, tm), BF16)
        elif j == 10:
            for hf in range(GROUP // 128):
                u_scr[hf] = zj[:, hf * 128:(hf + 1) * 128]
            for r in range(S5_CHUNK):
                for hf in range(GROUP // 128):
                    lo = r * GROUP + hf * 128
                    uf_ref[:, lo:lo + 128] = u_scr[hf, pl.ds(r, tm // S5_CHUNK, stride=S5_CHUNK), :].astype(BF16)


def _in_proj(x, mod, gain, w_bf, tabs, n_lat):
    b, t, d = x.shape
    n_out = w_bf.shape[1]
    tm = _pick_tile(t, 768, mult=128)
    tab_spec = pl.BlockSpec((tm, GROUP), lambda bi, m: (m, 0))
    va_rows = DA_V_DIM + DA_ONES_ROWS
    return pl.pallas_call(
        functools.partial(_in_proj_kernel, n_lat=n_lat, tm=tm, d=d),
        grid=(b, t // tm),
        in_specs=[pl.BlockSpec((None, tm, d), lambda bi, m: (bi, m, 0)),
                  pl.BlockSpec((None, 2, mod.shape[-1]), lambda bi, m: (bi, 0, 0)),
                  pl.BlockSpec((1, d), lambda bi, m: (0, 0)),
                  pl.BlockSpec((d, n_out), lambda bi, m: (0, 0)),
                  tab_spec, tab_spec, tab_spec, tab_spec],
        out_specs=[pl.BlockSpec((None, tm, n_out), lambda bi, m: (bi, m, 0)),
                   pl.BlockSpec((None, GROUP, tm), lambda bi, m: (bi, 0, m)),
                   pl.BlockSpec((None, DA_HEADS, va_rows, tm), lambda bi, m: (bi, 0, 0, m)),
                   pl.BlockSpec((None, tm // S5_CHUNK, S5_ROW), lambda bi, m: (bi, m, 0))],
        out_shape=[jax.ShapeDtypeStruct((b, t, n_out), BF16),
                   jax.ShapeDtypeStruct((b, GROUP, t), BF16),
                   jax.ShapeDtypeStruct((b, DA_HEADS, va_rows, t), BF16),
                   jax.ShapeDtypeStruct((b, t // S5_CHUNK, S5_ROW), BF16)],
        scratch_shapes=[pltpu.VMEM((GROUP // 128, tm, 128), F32)],
        compiler_params=_cparams(2),
    )(x, mod, gain, w_bf, *tabs)


def _rope_tables(n_lat, t, dim):
    q = dim // 4
    tok = jnp.arange(n_lat)
    row = (tok // GRID_W).astype(F32)
    col = (tok % GRID_W).astype(F32)
    inv = ROPE_BASE ** (-jnp.arange(q, dtype=F32) / q)
    lane = np.arange(GROUP) % dim
    axis, half, qi = lane // (2 * q), (lane % (2 * q)) // q, lane % q
    pos = jnp.where(jnp.asarray(axis)[None, :] == 0, row[:, None], col[:, None])
    ang = pos * inv[jnp.asarray(qi)][None, :]
    sign = jnp.asarray(np.where(half == 0, -1.0, 1.0), F32)[None, :]
    cos = jnp.concatenate([jnp.cos(ang), jnp.ones((t - n_lat, GROUP), F32)], axis=0)
    sin = jnp.concatenate([jnp.sin(ang) * sign, jnp.zeros((t - n_lat, GROUP), F32)], axis=0)
    return cos, sin


def _diff_attn_kernel(lam_ref, qt_ref, k_ref, vta_ref, gain_ref, o_ref, qm_scr, s_scr, mx_scr, m_scr, acc_scr,
                      *, n_lat, t, tq, tk, fin_scale):
    qi = pl.program_id(1)
    n_maps = 2 * DA_HEADS
    m_scr[...] = jnp.full(m_scr.shape, -jnp.inf, F32)
    acc_scr[...] = jnp.zeros(acc_scr.shape, F32)
    qt = qt_ref[...]
    row_map = lax.broadcasted_iota(jnp.int32, (GROUP, 1), 0) // DA_QK_DIM
    for i in range(n_maps):
        qm_scr[i] = qt * (row_map == i).astype(BF16)

    def scores(off, width, i, slot):
        s = jnp.dot(k_ref[pl.ds(off, width), :], qm_scr[i], preferred_element_type=F32)
        s_scr[slot, 0:width, :] = s
        mx_scr[slot] = jnp.max(s, axis=0, keepdims=True)

    def accumulate(off, width, i, slot):
        m_prev = m_scr[i]
        m_next = jnp.maximum(m_prev, mx_scr[slot])
        p = jnp.exp2(s_scr[slot, 0:width, :] - m_next).astype(BF16)
        va = vta_ref[i // 2, :, pl.ds(off, width)]
        acc_scr[i] = jnp.exp2(m_prev - m_next) * acc_scr[i] + jnp.dot(va, p, preferred_element_type=F32)
        m_scr[i] = m_next

    is_ctx = qi * tq >= n_lat
    n_chunks = t // tk

    @pl.when(jnp.logical_not(is_ctx))
    def _():
        scores(0, tk, 0, 0)

        def body(c, carry):
            off = pl.multiple_of(c * tk, tk)
            off_next = pl.multiple_of(jnp.minimum(c + 1, n_chunks - 1) * tk, tk)
            for i in range(n_maps):
                if i + 1 < n_maps:
                    scores(off, tk, i + 1, (i + 1) % 2)
                else:
                    scores(off_next, tk, 0, 0)
                accumulate(off, tk, i, i % 2)
            return carry

        lax.fori_loop(0, n_chunks, body, 0)

    @pl.when(is_ctx)
    def _():
        for i in range(n_maps):
            scores(n_lat, t - n_lat, i, i % 2)
            accumulate(n_lat, t - n_lat, i, i % 2)

    lam = lam_ref[0]
    outs = []
    for h in range(DA_HEADS):
        a0, a1 = acc_scr[2 * h], acc_scr[2 * h + 1]
        o = (a0[0:DA_V_DIM] / a0[DA_V_DIM:DA_V_DIM + 1]
             - lam * (a1[0:DA_V_DIM] / a1[DA_V_DIM:DA_V_DIM + 1]))
        inv = lax.rsqrt(jnp.mean(o * o, axis=0, keepdims=True) + NORM_EPS)
        outs.append(o * inv * gain_ref[...] * fin_scale)
    o_ref[...] = jnp.concatenate(outs, axis=0).T.astype(BF16)


def _diff_attention(z, qt, vta, lam, gain, n_lat, fin_scale):
    b, t, _ = z.shape
    tq = ATT_BLOCK
    tk = max(w for w in range(ATT_BLOCK, DA_MAX_KEY_CHUNK + 1, ATT_BLOCK) if t % w == 0)
    n_maps = 2 * DA_HEADS
    va_rows = vta.shape[2]
    return pl.pallas_call(
        functools.partial(_diff_attn_kernel, n_lat=n_lat, t=t, tq=tq, tk=tk, fin_scale=fin_scale),
        grid=(b, t // tq),
        in_specs=[pl.BlockSpec(memory_space=pltpu.SMEM),
                  pl.BlockSpec((None, GROUP, tq), lambda bi, m: (bi, 0, m)),
                  pl.BlockSpec((None, t, GROUP), lambda bi, m: (bi, 0, 1)),
                  pl.BlockSpec((None, DA_HEADS, va_rows, t), lambda bi, m: (bi, 0, 0, 0)),
                  pl.BlockSpec((DA_V_DIM, tq), lambda bi, m: (0, 0))],
        out_specs=pl.BlockSpec((None, tq, GROUP), lambda bi, m: (bi, m, 0)),
        out_shape=jax.ShapeDtypeStruct((b, t, GROUP), BF16),
        scratch_shapes=[pltpu.VMEM((n_maps, GROUP, tq), BF16), pltpu.VMEM((2, tk, tq), F32),
                        pltpu.VMEM((2, 1, tq), F32), pltpu.VMEM((n_maps, 1, tq), F32),
                        pltpu.VMEM((n_maps, va_rows, tq), F32)],
        compiler_params=_cparams(2),
    )(lam, qt, z, vta, gain)


def _retention_kernel(q_ref, k_ref, v_ref, g_ref, dm_ref, qw_ref, kw_ref, cd_ref, bd_ref, o_ref,
                      st_scr, of_scr, *, c, n_chunks, n_lat_chunks):
    d = pl.program_id(1)
    i = pl.program_id(2)
    chunk = jnp.where(d == 0, (i + n_lat_chunks) % n_chunks, n_chunks - 1 - i)

    @pl.when(i == 0)
    def _():
        st_scr[...] = jnp.zeros(st_scr.shape, F32)

    q, k, v = q_ref[...], k_ref[...], v_ref[...]
    lane_head = lax.broadcasted_iota(jnp.int32, (1, GROUP), 1) // RET_DIM
    o = jnp.dot((q.astype(F32) * qw_ref[...]).astype(BF16), st_scr[...].astype(BF16),
                preferred_element_type=F32)
    for h in range(RET_HEADS):
        mh = lane_head == h
        qh = q * mh.astype(BF16)
        s = lax.dot_general(qh, k, (((1,), (1,)), ((), ())), preferred_element_type=F32) * dm_ref[h]
        o = o + jnp.where(mh, jnp.dot(s.astype(BF16), v, preferred_element_type=F32), 0.0)
    kwk = (k.astype(F32) * kw_ref[...]).astype(BF16)
    kv = lax.dot_general(kwk, v, (((0,), (0,)), ((), ())), preferred_element_type=F32)
    st_scr[...] = cd_ref[...] * st_scr[...] + bd_ref[...] * kv
    off = pl.multiple_of(chunk * c, c)

    @pl.when(d == 0)
    def _():
        of_scr[pl.ds(off, c), :] = o

    @pl.when(d == 1)
    def _():
        tot = o + of_scr[pl.ds(off, c), :]
        sq = tot * tot
        inv = jnp.zeros(tot.shape, F32)
        for h in range(RET_HEADS):
            mh = lane_head == h
            ms = jnp.sum(jnp.where(mh, sq, 0.0), axis=-1, keepdims=True) * (1.0 / RET_DIM)
            inv = jnp.where(mh, lax.rsqrt(ms + NORM_EPS), inv)
        o_ref[...] = (tot * inv * jax.nn.silu(g_ref[...].astype(F32))).astype(BF16)


def _retention_tables(decay_logit, c):
    log_g = jax.nn.log_sigmoid(decay_logit.astype(F32))
    pos = jnp.arange(c, dtype=F32)
    rel = pos[:, None] - pos[None, :]
    rel = jnp.stack([rel, -rel])
    dm = jnp.where(rel[:, None] >= 0, jnp.exp(log_g[:, :, None, None] * jnp.maximum(rel[:, None], 0.0)), 0.0)
    lg_lane = jnp.repeat(log_g, RET_DIM, axis=1)
    q_exp = jnp.stack([pos + 1.0, c - pos])
    k_exp = jnp.stack([c - 1.0 - pos, pos])
    qw = jnp.exp(lg_lane[:, None, :] * q_exp[:, :, None])
    kw = jnp.exp(lg_lane[:, None, :] * k_exp[:, :, None])
    head = np.arange(GROUP) // RET_DIM
    bd = jnp.asarray((head[:, None] == head[None, :]).astype(np.float32))
    cd = jnp.exp(lg_lane * c)[:, :, None] * jnp.ones((1, 1, GROUP), F32)
    return dm, qw, kw, cd, bd


def _retention(z, tables, layer, n_lat):
    b, t, _ = z.shape
    c = ATT_BLOCK
    n_chunks, n_lat_chunks = t // c, n_lat // c
    dm, qw, kw, cd, bd = tables

    def chunk_of(d, i):
        return jnp.where(d == 0, (i + n_lat_chunks) % n_chunks, n_chunks - 1 - i)

    def zspec(col):
        return pl.BlockSpec((None, c, GROUP), lambda bi, d, i: (bi, chunk_of(d, i), col))

    return pl.pallas_call(
        functools.partial(_retention_kernel, c=c, n_chunks=n_chunks, n_lat_chunks=n_lat_chunks),
        grid=(b, 2, n_chunks),
        in_specs=[zspec(3), zspec(4), zspec(5), zspec(6),
                  pl.BlockSpec((None, None, RET_HEADS, c, c), lambda bi, d, i: (layer, d, 0, 0, 0)),
                  pl.BlockSpec((None, None, c, GROUP), lambda bi, d, i: (layer, d, 0, 0)),
                  pl.BlockSpec((None, None, c, GROUP), lambda bi, d, i: (layer, d, 0, 0)),
                  pl.BlockSpec((None, None, GROUP, GROUP), lambda bi, d, i: (layer, d, 0, 0)),
                  pl.BlockSpec((None, GROUP, GROUP), lambda bi, d, i: (layer, 0, 0))],
        out_specs=pl.BlockSpec((None, c, GROUP),
                               lambda bi, d, i: (bi, jnp.where(d == 0, n_chunks - 1, n_chunks - 1 - i), 0)),
        out_shape=jax.ShapeDtypeStruct((b, t, GROUP), BF16),
        scratch_shapes=[pltpu.VMEM((GROUP, GROUP), F32), pltpu.VMEM((t, GROUP), F32)],
        compiler_params=_cparams(3),
    )(z, z, z, z, dm, qw, kw, cd, bd)


def _nbr_attn_kernel(q_ref, kp_ref, kc_ref, kn_ref, kx_ref, vp_ref, vc_ref, vn_ref, vx_ref, tab_ref, o_ref):
    q = q_ref[...]
    k = jnp.concatenate([kp_ref[...], kc_ref[...], kn_ref[...], kx_ref[...]], axis=0)
    v = jnp.concatenate([vp_ref[...], vc_ref[...], vn_ref[...], vx_ref[...]], axis=0)
    lane_head = lax.broadcasted_iota(jnp.int32, (1, GROUP), 1) // NA_DIM
    o = jnp.zeros(q.shape, F32)
    for h in range(NA_HEADS):
        mh = lane_head == h
        qh = q * mh.astype(BF16)
        s = lax.dot_general(qh, k, (((1,), (1,)), ((), ())), preferred_element_type=F32) + tab_ref[h]
        p = jnp.exp(s - jnp.max(s, axis=-1, keepdims=True))
        l = jnp.sum(p, axis=-1, keepdims=True)
        o = o + jnp.where(mh, jnp.dot(p.astype(BF16), v, preferred_element_type=F32) / l, 0.0)
    o_ref[...] = o.astype(BF16)


def _nbr_tables(rpb, n_lat, n_ctx):
    blk_rows = ATT_BLOCK // GRID_W
    rows = n_lat // GRID_W
    n_blk = rows // blk_rows
    assert rows >= NA_WIN_ROWS and n_blk >= 3
    qcol = kcol = np.arange(GRID_W)
    c0 = np.clip(qcol - NA_WIN_COLS // 2, 0, GRID_W - NA_WIN_COLS)
    col_ok = (kcol[None, :] >= c0[:, None]) & (kcol[None, :] < c0[:, None] + NA_WIN_COLS)
    dc = np.clip(kcol[None, :] - qcol[:, None] + (NA_WIN_COLS - 1), 0, 2 * NA_WIN_COLS - 2)
    oh_c = np.eye(2 * NA_WIN_COLS - 1, dtype=np.float32)[dc]
    qr, kr = np.arange(blk_rows), np.arange(3 * blk_rows)
    oh_r, oks = [], []
    for m in (0, 1, n_blk - 1):
        qrow, krow = blk_rows * m + qr, blk_rows * (m - 1) + kr
        r0 = np.clip(qrow - NA_WIN_ROWS // 2, 0, rows - NA_WIN_ROWS)
        row_ok = (krow[None, :] >= r0[:, None]) & (krow[None, :] < r0[:, None] + NA_WIN_ROWS)
        dr = np.clip(krow[None, :] - qrow[:, None] + (NA_WIN_ROWS - 1), 0, 2 * NA_WIN_ROWS - 2)
        oh_r.append(np.eye(2 * NA_WIN_ROWS - 1, dtype=np.float32)[dr])
        oks.append((row_ok[:, None, :, None] & col_ok[None, :, None, :]).reshape(ATT_BLOCK, 3 * ATT_BLOCK))
    bias = jnp.einsum('vqka,hab,QKb->vhqQkK', jnp.asarray(np.stack(oh_r)), rpb.astype(F32), jnp.asarray(oh_c),
                      precision=lax.Precision.HIGHEST).reshape(3, NA_HEADS, ATT_BLOCK, 3 * ATT_BLOCK)
    tab = jnp.where(jnp.asarray(np.stack(oks))[:, None], bias, NEG_BIG)
    tab = jnp.concatenate([tab, jnp.full((1, NA_HEADS, ATT_BLOCK, 3 * ATT_BLOCK), NEG_BIG, F32)], axis=0)
    return jnp.concatenate([tab, jnp.zeros((4, NA_HEADS, ATT_BLOCK, n_ctx), F32)], axis=-1)


def _nbr_attention(z, tab, layer, n_lat):
    b, t, _ = z.shape
    n_ctx = t - n_lat
    assert n_ctx == ATT_BLOCK and n_lat % ATT_BLOCK == 0
    n_blk = n_lat // ATT_BLOCK

    def kv_specs(col):
        mk = lambda f: pl.BlockSpec((None, ATT_BLOCK, GROUP), lambda bi, m: (bi, f(m), col))
        return [mk(lambda m: jnp.clip(m - 1, 0, n_blk - 1)), mk(lambda m: jnp.minimum(m, n_blk - 1)),
                mk(lambda m: jnp.clip(m + 1, 0, n_blk - 1)), mk(lambda m: n_blk)]

    def variant(m):
        return jnp.where(m == 0, 0, jnp.where(m == n_blk - 1, 2, jnp.where(m == n_blk, 3, 1)))

    win = 3 * ATT_BLOCK + n_ctx
    return pl.pallas_call(
        _nbr_attn_kernel,
        grid=(b, n_blk + 1),
        in_specs=[pl.BlockSpec((None, ATT_BLOCK, GROUP), lambda bi, m: (bi, m, 7))] + kv_specs(8) + kv_specs(9)
                 + [pl.BlockSpec((None, None, NA_HEADS, ATT_BLOCK, win),
                                 lambda bi, m: (layer, variant(m), 0, 0, 0))],
        out_specs=pl.BlockSpec((None, ATT_BLOCK, GROUP), lambda bi, m: (bi, m, 0)),
        out_shape=jax.ShapeDtypeStruct((b, t, GROUP), BF16),
        compiler_params=_cparams(2),
    )(*([z] * 9), tab)


def _s5_kernel(u_ref, x0_ref, kb_ref, sm_ref, rm_ref, ap_ref, y_ref, xf_ref, st_scr, sloc_scr, xin_scr,
               *, rb):
    d = pl.program_id(0)
    i = pl.program_id(2)

    @pl.when(i == 0)
    def _():
        st_scr[...] = x0_ref[...]

    u = u_ref[...]
    sloc_scr[...] = jnp.dot(u, sm_ref[...], preferred_element_type=F32)
    ar, ai = ap_ref[0:1, :], ap_ref[1:2, :]
    ns = S5_NSTATE

    def step(n, carry):
        xr, xi = carry
        r = jnp.where(d == 0, n, rb - 1 - n)
        xin_scr[pl.ds(r, 1), 0:ns] = xr
        xin_scr[pl.ds(r, 1), ns:2 * ns] = xi
        s = sloc_scr[pl.ds(r, 1), :]
        return ar * xr - ai * xi + s[:, 0:ns], ar * xi + ai * xr + s[:, ns:2 * ns]

    xr, xi = lax.fori_loop(0, rb, step, (st_scr[:, 0:ns], st_scr[:, ns:2 * ns]))
    st_scr[:, 0:ns] = xr
    st_scr[:, ns:2 * ns] = xi
    y_state = jnp.dot(xin_scr[...].astype(BF16), rm_ref[...], preferred_element_type=F32)
    for direction in range(2):
        @pl.when(d == direction)
        def _():
            for t_out in range(S5_CHUNK):
                srcs = range(0, t_out + 1) if direction == 0 else range(t_out, S5_CHUNK)
                acc = y_state[:, t_out * GROUP:(t_out + 1) * GROUP]
                for t_in in srcs:
                    acc = acc + jnp.dot(u[:, t_in * GROUP:(t_in + 1) * GROUP], kb_ref[abs(t_out - t_in)],
                                        preferred_element_type=F32)
                y_ref[:, t_out * GROUP:(t_out + 1) * GROUP] = acc
    xf_ref[...] = st_scr[...]


def _s5_operators(a_re, a_im, b_re, b_im, c_re, c_im, log_step):
    tc, g, p, ch = S5_CHUNK, S5_GROUPS, S5_STATE, S5_CH
    hi = lax.Precision.HIGHEST
    ar, ai = a_re.astype(F32), a_im.astype(F32)
    dt = jnp.exp(log_step.astype(F32))[..., None]
    steps = jnp.arange(tc + 1, dtype=F32)[None, :, None, None]
    mag = jnp.exp((ar * dt)[:, None] * steps)
    ang = (ai * dt)[:, None] * steps
    pw_r, pw_i = mag * jnp.cos(ang), mag * jnp.sin(ang)
    x, y, den = pw_r[:, 1] - 1.0, pw_i[:, 1], ar * ar + ai * ai
    cf_r, cf_i = ((x * ar + y * ai) / den)[..., None], ((y * ar - x * ai) / den)[..., None]
    bre, bim = b_re.astype(F32), b_im.astype(F32)
    bb_r, bb_i = cf_r * bre - cf_i * bim, cf_r * bim + cf_i * bre
    cr, ci = c_re.astype(F32), c_im.astype(F32)
    w_r = pw_r[..., None] * bb_r[:, None] - pw_i[..., None] * bb_i[:, None]
    w_i = pw_r[..., None] * bb_i[:, None] + pw_i[..., None] * bb_r[:, None]
    kern = (jnp.einsum('dgcp,dtgph->dtghc', cr, w_r, precision=hi)
            - jnp.einsum('dgcp,dtgph->dtghc', ci, w_i, precision=hi))
    eye = jnp.eye(g, dtype=BF16)
    idx = np.arange(tc)
    kb = (kern[:, 0:tc, :, :, None, :].astype(BF16)
          * eye[None, None, :, None, :, None]).reshape(2, tc, g * ch, g * ch)
    outs = []
    for d in range(2):
        f_exp = jnp.asarray((tc - 1 - idx) if d == 0 else idx)
        sm = jnp.concatenate(
            [(jnp.transpose(part[d][f_exp], (0, 1, 3, 2)).astype(BF16)[:, :, :, None, :]
              * eye[None, :, None, :, None]).reshape(tc * g * ch, g * p) for part in (w_r, w_i)], axis=1)
        e_exp = jnp.asarray((idx + 1) if d == 0 else (tc - idx))
        pr, pi_ = pw_r[d][e_exp][:, :, None, :], pw_i[d][e_exp][:, :, None, :]
        ca_r, ca_i = cr[d][None] * pr - ci[d][None] * pi_, cr[d][None] * pi_ + ci[d][None] * pr
        rm = jnp.concatenate(
            [(jnp.transpose(part, (1, 3, 0, 2)).astype(BF16)[:, :, :, None, :]
              * eye[:, None, None, :, None]).reshape(g * p, tc * g * ch) for part in (ca_r, -ca_i)], axis=0)
        ap = jnp.stack([pw_r[d, tc].reshape(-1), pw_i[d, tc].reshape(-1)])
        outs.append((sm, rm, ap))
    sm, rm, ap = (jnp.stack(v) for v in zip(*outs))
    return kb, sm, rm, ap


def _s5_scan(u_rows, row0, r, x0, ops, layer, rb):
    b = u_rows.shape[0]
    nb = r // rb
    assert r % rb == 0 and row0 % rb == 0
    kb, sm, rm, ap = ops
    blk = lambda d, i: jnp.where(d == 0, i, nb - 1 - i)
    op_spec = lambda shp: pl.BlockSpec((None, None) + shp, lambda d, bi, i: (layer, d) + (0,) * len(shp),
                                       pipeline_mode=pl.Buffered(1))
    return pl.pallas_call(
        functools.partial(_s5_kernel, rb=rb),
        grid=(2, b, nb),
        in_specs=[pl.BlockSpec((None, rb, S5_ROW), lambda d, bi, i: (bi, row0 // rb + blk(d, i), 0)),
                  pl.BlockSpec((None, None, 1, 2 * S5_NSTATE), lambda d, bi, i: (d, bi, 0, 0)),
                  op_spec((S5_CHUNK, GROUP, GROUP)), op_spec((S5_ROW, 2 * S5_NSTATE)),
                  op_spec((2 * S5_NSTATE, S5_ROW)),
                  pl.BlockSpec((None, None, 2, S5_NSTATE), lambda d, bi, i: (layer, d, 0, 0))],
        out_specs=[pl.BlockSpec((None, None, rb, S5_ROW), lambda d, bi, i: (d, bi, blk(d, i), 0)),
                   pl.BlockSpec((None, None, 1, 2 * S5_NSTATE), lambda d, bi, i: (d, bi, 0, 0))],
        out_shape=[jax.ShapeDtypeStruct((2, b, r, S5_ROW), F32),
                   jax.ShapeDtypeStruct((2, b, 1, 2 * S5_NSTATE), F32)],
        scratch_shapes=[pltpu.VMEM((1, 2 * S5_NSTATE), F32), pltpu.VMEM((rb, 2 * S5_NSTATE), F32),
                        pltpu.VMEM((rb, 2 * S5_NSTATE), F32)],
        compiler_params=_cparams(3),
    )(u_rows, x0, kb, sm, rm, ap)


def _s5_glu_kernel(y_ref, u_ref, d_ref, w_ref, b_ref, o_ref, y_scr, *, rb):
    yf = y_ref[0] + y_ref[1]
    for r in range(S5_CHUNK):
        for hf in range(GROUP // 128):
            lo = r * GROUP + hf * 128
            y_scr[hf, pl.ds(r, rb, stride=S5_CHUNK), :] = yf[:, lo:lo + 128]
    y = (u_ref[...].astype(F32) * d_ref[...]
         + jnp.concatenate([y_scr[hf] for hf in range(GROUP // 128)], axis=-1))
    act = jax.nn.gelu(y)
    gate = jnp.dot(act.astype(BF16), w_ref[...], preferred_element_type=F32) + b_ref[...]
    o_ref[...] = (act * jax.nn.sigmoid(gate)).astype(BF16)


def _s5_glu(y, z, tok0, rb, d_skip, glu_w, glu_b):
    b, r = y.shape[1], y.shape[2]
    tm = rb * S5_CHUNK
    assert r % rb == 0 and tok0 % tm == 0
    return pl.pallas_call(
        functools.partial(_s5_glu_kernel, rb=rb),
        grid=(b, r // rb),
        in_specs=[pl.BlockSpec((2, None, rb, S5_ROW), lambda bi, m: (0, bi, m, 0)),
                  pl.BlockSpec((None, tm, GROUP), lambda bi, m: (bi, tok0 // tm + m, 10)),
                  pl.BlockSpec((1, GROUP), lambda bi, m: (0, 0)),
                  pl.BlockSpec((GROUP, GROUP), lambda bi, m: (0, 0)),
                  pl.BlockSpec((1, GROUP), lambda bi, m: (0, 0))],
        out_specs=pl.BlockSpec((None, tm, GROUP), lambda bi, m: (bi, m, 0)),
        out_shape=jax.ShapeDtypeStruct((b, r * S5_CHUNK, GROUP), BF16),
        scratch_shapes=[pltpu.VMEM((GROUP // 128, tm, 128), F32)],
        compiler_params=_cparams(2),
    )(y, z, d_skip, glu_w, glu_b)


def _s5_mixer(z, u_rows, ops, layer, d_skip, glu_w, glu_b, n_lat):
    b, t, _ = z.shape
    r_lat, r_ctx = n_lat // S5_CHUNK, (t - n_lat) // S5_CHUNK
    zero = jnp.zeros((2, b, 1, 2 * S5_NSTATE), F32)
    y_ctx, x_ctx = _s5_scan(u_rows, r_lat, r_ctx, zero, ops, layer, r_ctx)
    y_lat, _ = _s5_scan(u_rows, 0, r_lat, x_ctx, ops, layer, _pick_tile(r_lat, 256))
    glu = (d_skip.reshape(1, GROUP).astype(F32), glu_w.astype(BF16), glu_b.reshape(1, GROUP).astype(F32))
    s_lat = _s5_glu(y_lat, z, 0, _pick_tile(r_lat, 128), *glu)
    s_ctx = _s5_glu(y_ctx, z, n_lat, r_ctx, *glu)
    return jnp.concatenate([s_lat, s_ctx], axis=1)


def _out_proj_kernel(a_ref, r_ref, n_ref, s_ref, w_ref, x_ref, mod_ref, gain_ref, o_ref, *, n_lat, tm, d):
    m = pl.program_id(1)
    mix = jnp.dot(a_ref[...], w_ref[0:GROUP, :], preferred_element_type=F32)
    for j, ref in enumerate((r_ref, n_ref, s_ref), start=1):
        mix = mix + jnp.dot(ref[...], w_ref[j * GROUP:(j + 1) * GROUP, :], preferred_element_type=F32)
    gate = _row_mod(mod_ref, 2, m, tm, n_lat, d)
    o_ref[...] = x_ref[...] + gate * (_rms_rows(mix) * gain_ref[...])


def _out_proj(parts, w_bf, x, mod, gain, n_lat):
    b, t, d = x.shape
    tm = _pick_tile(t, 528)
    part_spec = pl.BlockSpec((None, tm, GROUP), lambda bi, m: (bi, m, 0))
    row_spec = pl.BlockSpec((None, tm, d), lambda bi, m: (bi, m, 0))
    return pl.pallas_call(
        functools.partial(_out_proj_kernel, n_lat=n_lat, tm=tm, d=d),
        grid=(b, t // tm),
        in_specs=[part_spec] * 4 + [pl.BlockSpec((d, d), lambda bi, m: (0, 0)), row_spec,
                                    pl.BlockSpec((None, 2, mod.shape[-1]), lambda bi, m: (bi, 0, 0)),
                                    pl.BlockSpec((1, d), lambda bi, m: (0, 0))],
        out_specs=row_spec,
        out_shape=jax.ShapeDtypeStruct((b, t, d), F32),
        compiler_params=_cparams(2),
    )(*parts, w_bf, x, mod, gain)


def _moe_norm_kernel(x_ref, mod_ref, gain_ref, router_ref, h_ref, route_ref, *, n_lat, tm, d):
    m = pl.program_id(1)
    shift = _row_mod(mod_ref, 3, m, tm, n_lat, d)
    scale = _row_mod(mod_ref, 4, m, tm, n_lat, d)
    h = (_rms_rows(x_ref[...]) * gain_ref[...]) * (1.0 + scale) + shift
    h_ref[...] = h.astype(BF16)
    logits = jnp.dot(h, router_ref[...], precision=lax.Precision.HIGHEST, preferred_element_type=F32)
    lane = lax.broadcasted_iota(jnp.int32, logits.shape, 1)
    logits = jnp.where(lane < MOE_EXPERTS, logits, -jnp.inf)
    v1 = jnp.max(logits, axis=-1, keepdims=True)
    i1 = jnp.min(jnp.where(logits == v1, lane, 128), axis=-1, keepdims=True)
    rest_l = jnp.where(lane == i1, -jnp.inf, logits)
    v2 = jnp.max(rest_l, axis=-1, keepdims=True)
    i2 = jnp.min(jnp.where(rest_l == v2, lane, 128), axis=-1, keepdims=True)
    e = jnp.exp(v2 - v1)
    w1 = 1.0 / (1.0 + e)
    w2 = e / (1.0 + e)
    route = jnp.where(lane == 0, w1, jnp.where(lane == 1, w2, jnp.where(
        lane == 2, i1.astype(F32), jnp.where(lane == 3, i2.astype(F32), 0.0))))
    route_ref[...] = route


def _moe_norm(x, mod, gain, router, n_lat):
    b, t, d = x.shape
    tm = _pick_tile(t, 528)
    row_spec = pl.BlockSpec((None, tm, d), lambda bi, m: (bi, m, 0))
    return pl.pallas_call(
        functools.partial(_moe_norm_kernel, n_lat=n_lat, tm=tm, d=d),
        grid=(b, t // tm),
        in_specs=[row_spec, pl.BlockSpec((None, 2, mod.shape[-1]), lambda bi, m: (bi, 0, 0)),
                  pl.BlockSpec((1, d), lambda bi, m: (0, 0)), pl.BlockSpec((d, 128), lambda bi, m: (0, 0))],
        out_specs=[row_spec, pl.BlockSpec((None, tm, 128), lambda bi, m: (bi, m, 0))],
        out_shape=[jax.ShapeDtypeStruct((b, t, d), BF16), jax.ShapeDtypeStruct((b, t, 128), F32)],
        compiler_params=_cparams(2),
    )(x, mod, gain, jnp.pad(router.astype(F32), ((0, 0), (0, 128 - MOE_EXPERTS))))


def _swiglu_kernel(te_ref, nu_ref, x_ref, wg_ref, wu_ref, wd_ref, o_ref, acc_scr):
    tile = pl.program_id(0)
    f = pl.program_id(1)
    nf = pl.num_programs(1)
    used = tile < nu_ref[0]

    @pl.when(f == 0)
    def _():
        acc_scr[...] = jnp.zeros(acc_scr.shape, F32)

    @pl.when(used)
    def _():
        x = x_ref[...]
        g = jnp.dot(x, wg_ref[...], preferred_element_type=F32)
        u = jnp.dot(x, wu_ref[...], preferred_element_type=F32)
        acc_scr[...] += jnp.dot((jax.nn.silu(g) * u).astype(BF16), wd_ref[...], preferred_element_type=F32)

    @pl.when(f == nf - 1)
    def _():
        o_ref[...] = acc_scr[...].astype(BF16)


def _swiglu_grouped(xs, tile_expert, n_used, w_gate, w_up, w_down, tm):
    p, d = xs.shape
    n_exp, _, f = w_gate.shape
    tf = f // 2
    grid_spec = pltpu.PrefetchScalarGridSpec(
        num_scalar_prefetch=2,
        grid=(p // tm, f // tf),
        in_specs=[pl.BlockSpec((tm, d), lambda t, j, te, nu: (t, 0)),
                  pl.BlockSpec((None, d, tf), lambda t, j, te, nu: (te[t], 0, j)),
                  pl.BlockSpec((None, d, tf), lambda t, j, te, nu: (te[t], 0, j)),
                  pl.BlockSpec((None, tf, d), lambda t, j, te, nu: (te[t], j, 0))],
        out_specs=pl.BlockSpec((tm, d), lambda t, j, te, nu: (t, 0)),
        scratch_shapes=[pltpu.VMEM((tm, d), F32)])
    return pl.pallas_call(
        _swiglu_kernel, grid_spec=grid_spec,
        out_shape=jax.ShapeDtypeStruct((p, d), BF16),
        compiler_params=_cparams(2),
    )(tile_expert, n_used, xs, w_gate, w_up, w_down)


def _moe_residual_kernel(y1_ref, y2_ref, route_ref, x_ref, mod_ref, gain_ref, o_ref, *, n_lat, tm, d):
    m = pl.program_id(1)
    y = route_ref[:, 0:1] * y1_ref[...].astype(F32) + route_ref[:, 1:2] * y2_ref[...].astype(F32)
    gate = _row_mod(mod_ref, 5, m, tm, n_lat, d)
    o_ref[...] = x_ref[...] + gate * (_rms_rows(y) * gain_ref[...])


def _moe_residual(y12, route, x, mod, gain, n_lat, rows):
    b, t, d = x.shape
    tm = _pick_tile(rows, 528)
    row_spec = pl.BlockSpec((None, tm, d), lambda bi, m: (bi, m, 0))
    return pl.pallas_call(
        functools.partial(_moe_residual_kernel, n_lat=n_lat, tm=tm, d=d),
        grid=(b, rows // tm),
        in_specs=[pl.BlockSpec((None, None, tm, d), lambda bi, m: (0, bi, m, 0)),
                  pl.BlockSpec((None, None, tm, d), lambda bi, m: (1, bi, m, 0)),
                  pl.BlockSpec((None, tm, 128), lambda bi, m: (bi, m, 0)), row_spec,
                  pl.BlockSpec((None, 2, mod.shape[-1]), lambda bi, m: (bi, 0, 0)),
                  pl.BlockSpec((1, d), lambda bi, m: (0, 0))],
        out_specs=row_spec,
        out_shape=jax.ShapeDtypeStruct((b, rows, d), F32),
        compiler_params=_cparams(2),
    )(y12, y12, route, x, mod, gain)


def _dense_ffn_kernel(x_ref, mod_ref, gin_ref, gout_ref, wg_ref, wu_ref, wd_ref, o_ref, h_scr, acc_scr,
                      *, n_lat, tm, d):
    m = pl.program_id(1)
    f = pl.program_id(2)

    @pl.when(f == 0)
    def _():
        shift = _row_mod(mod_ref, 3, m, tm, n_lat, d)
        scale = _row_mod(mod_ref, 4, m, tm, n_lat, d)
        h_scr[...] = ((_rms_rows(x_ref[...]) * gin_ref[...]) * (1.0 + scale) + shift).astype(BF16)
        acc_scr[...] = jnp.zeros(acc_scr.shape, F32)

    h = h_scr[...]
    g = jnp.dot(h, wg_ref[...], preferred_element_type=F32)
    u = jnp.dot(h, wu_ref[...], preferred_element_type=F32)
    acc_scr[...] += jnp.dot((jax.nn.silu(g) * u).astype(BF16), wd_ref[...], preferred_element_type=F32)

    @pl.when(f == pl.num_programs(2) - 1)
    def _():
        gate = _row_mod(mod_ref, 5, m, tm, n_lat, d)
        o_ref[...] = x_ref[...] + gate * (_rms_rows(acc_scr[...]) * gout_ref[...])


def _dense_ffn(x, mod, gain_in, gain_out, w_gate, w_up, w_down, n_lat):
    b, t, d = x.shape
    f = w_gate.shape[1]
    tm = _pick_tile(t, 528)
    tf = f // 2
    row_spec = pl.BlockSpec((None, tm, d), lambda bi, m, j: (bi, m, 0))
    vec_spec = pl.BlockSpec((1, d), lambda bi, m, j: (0, 0))
    return pl.pallas_call(
        functools.partial(_dense_ffn_kernel, n_lat=n_lat, tm=tm, d=d),
        grid=(b, t // tm, f // tf),
        in_specs=[row_spec, pl.BlockSpec((None, 2, mod.shape[-1]), lambda bi, m, j: (bi, 0, 0)),
                  vec_spec, vec_spec,
                  pl.BlockSpec((d, tf), lambda bi, m, j: (0, j)), pl.BlockSpec((d, tf), lambda bi, m, j: (0, j)),
                  pl.BlockSpec((tf, d), lambda bi, m, j: (j, 0))],
        out_specs=row_spec,
        out_shape=jax.ShapeDtypeStruct((b, t, d), F32),
        scratch_shapes=[pltpu.VMEM((tm, d), BF16), pltpu.VMEM((tm, d), F32)],
        compiler_params=_cparams(3),
    )(x, mod, gain_in, gain_out, w_gate, w_up, w_down)


def _moe_ffn(h, route, w_gate, w_up, w_down):
    b, t, d = h.shape
    n = b * t
    tm = _pick_tile(2 * n, 512)
    route = route.reshape(n, 128)
    expert = jnp.concatenate([route[:, 2], route[:, 3]]).astype(jnp.int32)
    token = jnp.concatenate([jnp.arange(n, dtype=jnp.int32)] * 2)
    onehot = (expert[:, None] == jnp.arange(MOE_EXPERTS, dtype=jnp.int32)[None, :]).astype(jnp.int32)
    counts = jnp.sum(onehot, axis=0)
    rank = jnp.sum((jnp.cumsum(onehot, axis=0) - onehot) * onehot, axis=1)
    padded = ((counts + tm - 1) // tm) * tm
    ends = jnp.cumsum(padded)
    pos = jnp.sum(onehot * (ends - padded)[None, :], axis=1) + rank
    n_rows = 2 * n + MOE_EXPERTS * tm
    n_tiles = n_rows // tm
    tile_start = jnp.arange(n_tiles, dtype=jnp.int32) * tm
    tile_expert = jnp.minimum(jnp.sum((ends[None, :] <= tile_start[:, None]).astype(jnp.int32), axis=1),
                              MOE_EXPERTS - 1)
    n_used = (ends[-1] // tm).astype(jnp.int32).reshape(1)
    row_token = jnp.zeros((n_rows,), jnp.int32).at[pos].set(token)
    xs = jnp.take(jnp.pad(h.reshape(n, d), ((0, n_rows - n), (0, 0))), row_token, axis=0)
    out = _swiglu_grouped(xs, tile_expert, n_used, w_gate, w_up, w_down, tm)
    return jnp.take(out, pos, axis=0).reshape(2, b, t, d)


def kernel(x, c, ctx, c_ctx, w_mod, b_mod, norm_gain, w_in, w_out, da_lambda, da_norm_gain,
           ret_decay_logit, na_rpb, s5_a_re, s5_a_im, s5_b_re, s5_b_im, s5_c_re, s5_c_im,
           s5_log_step, s5_d, s5_glu_w, s5_glu_b, ffn_w_gate, ffn_w_up, ffn_w_down,
           moe_router, moe_w_gate, moe_w_up, moe_w_down):
    b, n_lat, d = x.shape
    n_ctx = ctx.shape[1]
    t = n_lat + n_ctx
    depth = w_mod.shape[0]
    assert d == 4 * GROUP and b + 1 <= 8

    c_rows = jnp.zeros((8, d), F32).at[:b].set(c).at[b].set(c_ctx)
    mod_all = _modulation(c_rows, w_mod, b_mod)
    tabs = _rope_tables(n_lat, t, DA_QK_DIM) + _rope_tables(n_lat, t, RET_DIM)
    h_all = jnp.concatenate([x, ctx], axis=1)

    lam_inits = [0.8 - 0.6 * math.exp(-0.3 * layer) for layer in range(depth)]
    lp = da_lambda.astype(F32)
    lams = (jnp.exp(jnp.sum(lp[:, 0] * lp[:, 1], axis=-1)) - jnp.exp(jnp.sum(lp[:, 2] * lp[:, 3], axis=-1))
            + jnp.asarray(lam_inits, F32))
    ret_tabs = jax.vmap(lambda dl: _retention_tables(dl, ATT_BLOCK))(ret_decay_logit)
    na_tabs = jax.vmap(lambda r: _nbr_tables(r, n_lat, n_ctx))(na_rpb)
    s5_ops = jax.vmap(_s5_operators)(s5_a_re, s5_a_im, s5_b_re, s5_b_im, s5_c_re, s5_c_im, s5_log_step)

    for layer in range(depth):
        mod = jnp.stack([mod_all[layer, :b], jnp.broadcast_to(mod_all[layer, b], (b, 6 * d))], axis=1)
        g = norm_gain[layer].astype(F32)
        lam_init = lam_inits[layer]

        z, qt, vta, u_rows = _in_proj(h_all, mod, g[0:1], w_in[layer].astype(BF16), tabs, n_lat)
        da_gain = jnp.broadcast_to(da_norm_gain[layer].astype(F32)[:, None], (DA_V_DIM, ATT_BLOCK))
        a_out = _diff_attention(z, qt, vta, lams[layer].reshape(1), da_gain, n_lat, 1.0 - lam_init)
        r_out = _retention(z, ret_tabs, layer, n_lat)
        n_out = _nbr_attention(z, na_tabs, layer, n_lat)
        s_out = _s5_mixer(z, u_rows, s5_ops, layer, s5_d[layer], s5_glu_w[layer], s5_glu_b[layer], n_lat)
        h_all = _out_proj((a_out, r_out, n_out, s_out), w_out[layer].astype(BF16), h_all, mod, g[1:2], n_lat)

        i = layer // 2
        if layer % 2 == 0:
            h_all = _dense_ffn(h_all, mod, g[2:3], g[3:4], ffn_w_gate[i].astype(BF16), ffn_w_up[i].astype(BF16),
                               ffn_w_down[i].astype(BF16), n_lat)
        else:
            f_in, route = _moe_norm(h_all, mod, g[2:3], moe_router[i], n_lat)
            y12 = _moe_ffn(f_in, route, moe_w_gate[i].astype(BF16), moe_w_up[i].astype(BF16),
                           moe_w_down[i].astype(BF16))
            h_all = _moe_residual(y12, route, h_all, mod, g[3:4], n_lat, n_lat if layer == depth - 1 else t)
    return h_all[:, :n_lat]
```

```python
import functools
import math

import numpy as np
import jax
import jax.numpy as jnp
from jax import lax
from jax.experimental import pallas as pl
from jax.experimental.pallas import tpu as pltpu

F32 = jnp.float32
BF16 = jnp.bfloat16

GRID_W = 64
GROUP = 256
N_IN_SLICES = 11
DA_HEADS, DA_QK_DIM, DA_V_DIM = 4, 32, 64
DA_ONES_ROWS = 16
DA_MAX_KEY_CHUNK = 4224
RET_HEADS, RET_DIM = 4, 64
NA_HEADS, NA_DIM, NA_WIN_ROWS, NA_WIN_COLS = 4, 64, 8, 16
S5_CH, S5_GROUPS, S5_STATE = 16, 16, 64
MOE_EXPERTS, MOE_TOP_K = 8, 2
ROPE_BASE = 10000.0
NORM_EPS = 1e-6
NEG_BIG = -1e30

S5_CHUNK = 8
S5_ROW = S5_CHUNK * GROUP
S5_NSTATE = S5_GROUPS * S5_STATE
ATT_BLOCK = 256
VMEM_LIMIT = 48 * 1024 * 1024


def _cparams(n_axes):
    return pltpu.CompilerParams(dimension_semantics=("arbitrary",) * n_axes,
                                vmem_limit_bytes=VMEM_LIMIT)


def _pick_tile(n, target, mult=16):
    best = None
    for t in range(mult, min(n, target) + 1, mult):
        if n % t == 0:
            best = t
    assert best is not None, (n, target)
    return best


def _rms_rows(x):
    return x * lax.rsqrt(jnp.mean(x * x, axis=-1, keepdims=True) + NORM_EPS)


def _row_mod(mod_ref, k, m, tm, n_lat, d):
    row = m * tm + lax.broadcasted_iota(jnp.int32, (tm, 1), 0)
    return jnp.where(row >= n_lat, mod_ref[1:2, k * d:(k + 1) * d], mod_ref[0:1, k * d:(k + 1) * d])


def _mod_kernel(c_ref, w_ref, b_ref, o_ref):
    o_ref[...] = jnp.dot(jax.nn.silu(c_ref[...]), w_ref[...], precision=lax.Precision.HIGHEST,
                         preferred_element_type=F32) + b_ref[...]


def _modulation(c_rows, w_mod, b_mod):
    depth, d, n6 = w_mod.shape
    tn = 1024
    return pl.pallas_call(
        _mod_kernel,
        grid=(depth, n6 // tn),
        in_specs=[pl.BlockSpec((8, d), lambda l, n: (0, 0)),
                  pl.BlockSpec((None, d, tn), lambda l, n: (l, 0, n)),
                  pl.BlockSpec((None, 1, tn), lambda l, n: (l, 0, n))],
        out_specs=pl.BlockSpec((None, 8, tn), lambda l, n: (l, 0, n)),
        out_shape=jax.ShapeDtypeStruct((depth, 8, n6), F32),
        compiler_params=_cparams(2),
    )(c_rows, w_mod, b_mod.reshape(depth, 1, n6))


def _rope(z, cos, sin_signed, q):
    n = z.shape[-1]
    lane = lax.broadcasted_iota(jnp.int32, z.shape, 1)
    first = (lane % (2 * q)) < q
    partner = jnp.where(first, pltpu.roll(z, n - q, 1), pltpu.roll(z, q, 1))
    return z * cos + partner * sin_signed


def _in_proj_kernel(x_ref, mod_ref, gain_ref, w_ref, cda_ref, sda_ref, crt_ref, srt_ref,
                    z_ref, qt_ref, vta_ref, uf_ref, u_scr, *, n_lat, tm, d):
    m = pl.program_id(1)
    shift = _row_mod(mod_ref, 0, m, tm, n_lat, d)
    scale = _row_mod(mod_ref, 1, m, tm, n_lat, d)
    h = ((_rms_rows(x_ref[...]) * gain_ref[...]) * (1.0 + scale) + shift).astype(BF16)
    col_scale = {0: DA_QK_DIM ** -0.5 * math.log2(math.e), 4: RET_DIM ** -0.5, 7: NA_DIM ** -0.5}
    for j in range(N_IN_SLICES):
        zj = jnp.dot(h, w_ref[:, j * GROUP:(j + 1) * GROUP], preferred_element_type=F32)
        if j in (0, 1):
            zj = _rope(zj, cda_ref[...], sda_ref[...], DA_QK_DIM // 4)
        elif j in (3, 4):
            zj = _rope(zj, crt_ref[...], srt_ref[...], RET_DIM // 4)
        if j in col_scale:
            zj = zj * col_scale[j]
        z_ref[:, j * GROUP:(j + 1) * GROUP] = zj.astype(BF16)
        if j == 0:
            qt_ref[...] = zj.T.astype(BF16)
        elif j == 2:
            vta_ref[:, 0:DA_V_DIM, :] = zj.T.reshape(DA_HEADS, DA_V_DIM, tm).astype(BF16)
            vta_ref[:, DA_V_DIM:, :] = jnp.ones((DA_HEADS, DA_ONES_ROWS, tm), BF16)
        elif j == 10:
            for hf in range(GROUP // 128):
                u_scr[hf] = zj[:, hf * 128:(hf + 1) * 128]
            for r in range(S5_CHUNK):
                for hf in range(GROUP // 128):
                    lo = r * GROUP + hf * 128
                    uf_ref[:, lo:lo + 128] = u_scr[hf, pl.ds(r, tm // S5_CHUNK, stride=S5_CHUNK), :].astype(BF16)


def _in_proj(x, mod, gain, w_bf, tabs, n_lat):
    b, t, d = x.shape
    n_out = w_bf.shape[1]
    tm = _pick_tile(t, 768, mult=128)
    tab_spec = pl.BlockSpec((tm, GROUP), lambda bi, m: (m, 0))
    va_rows = DA_V_DIM + DA_ONES_ROWS
    return pl.pallas_call(
        functools.partial(_in_proj_kernel, n_lat=n_lat, tm=tm, d=d),
        grid=(b, t // tm),
        in_specs=[pl.BlockSpec((None, tm, d), lambda bi, m: (bi, m, 0)),
                  pl.BlockSpec((None, 2, mod.shape[-1]), lambda bi, m: (bi, 0, 0)),
                  pl.BlockSpec((1, d), lambda bi, m: (0, 0)),
                  pl.BlockSpec((d, n_out), lambda bi, m: (0, 0)),
                  tab_spec, tab_spec, tab_spec, tab_spec],
        out_specs=[pl.BlockSpec((None, tm, n_out), lambda bi, m: (bi, m, 0)),
                   pl.BlockSpec((None, GROUP, tm), lambda bi, m: (bi, 0, m)),
                   pl.BlockSpec((None, DA_HEADS, va_rows, tm), lambda bi, m: (bi, 0, 0, m)),
                   pl.BlockSpec((None, tm // S5_CHUNK, S5_ROW), lambda bi, m: (bi, m, 0))],
        out_shape=[jax.ShapeDtypeStruct((b, t, n_out), BF16),
                   jax.ShapeDtypeStruct((b, GROUP, t), BF16),
                   jax.ShapeDtypeStruct((b, DA_HEADS, va_rows, t), BF16),
                   jax.ShapeDtypeStruct((b, t // S5_CHUNK, S5_ROW), BF16)],
        scratch_shapes=[pltpu.VMEM((GROUP // 128, tm, 128), F32)],
        compiler_params=_cparams(2),
    )(x, mod, gain, w_bf, *tabs)


def _rope_tables(n_lat, t, dim):
    q = dim // 4
    tok = jnp.arange(n_lat)
    row = (tok // GRID_W).astype(F32)
    col = (tok % GRID_W).astype(F32)
    inv = ROPE_BASE ** (-jnp.arange(q, dtype=F32) / q)
    lane = np.arange(GROUP) % dim
    axis, half, qi = lane // (2 * q), (lane % (2 * q)) // q, lane % q
    pos = jnp.where(jnp.asarray(axis)[None, :] == 0, row[:, None], col[:, None])
    ang = pos * inv[jnp.asarray(qi)][None, :]
    sign = jnp.asarray(np.where(half == 0, -1.0, 1.0), F32)[None, :]
    cos = jnp.concatenate([jnp.cos(ang), jnp.ones((t - n_lat, GROUP), F32)], axis=0)
    sin = jnp.concatenate([jnp.sin(ang) * sign, jnp.zeros((t - n_lat, GROUP), F32)], axis=0)
    return cos, sin


def _diff_attn_kernel(lam_ref, qt_ref, k_ref, vta_ref, gain_ref, o_ref, qm_scr, s_scr, mx_scr, m_scr, acc_scr,
                      *, n_lat, t, tq, tk, fin_scale):
    qi = pl.program_id(1)
    n_maps = 2 * DA_HEADS
    m_scr[...] = jnp.full(m_scr.shape, -jnp.inf, F32)
    acc_scr[...] = jnp.zeros(acc_scr.shape, F32)
    qt = qt_ref[...]
    row_map = lax.broadcasted_iota(jnp.int32, (GROUP, 1), 0) // DA_QK_DIM
    for i in range(n_maps):
        qm_scr[i] = qt * (row_map == i).astype(BF16)

    def scores(off, width, i, slot):
        s = jnp.dot(k_ref[pl.ds(off, width), :], qm_scr[i], preferred_element_type=F32)
        s_scr[slot, 0:width, :] = s
        mx_scr[slot] = jnp.max(s, axis=0, keepdims=True)

    def accumulate(off, width, i, slot):
        m_prev = m_scr[i]
        m_next = jnp.maximum(m_prev, mx_scr[slot])
        p = jnp.exp2(s_scr[slot, 0:width, :] - m_next).astype(BF16)
        va = vta_ref[i // 2, :, pl.ds(off, width)]
        acc_scr[i] = jnp.exp2(m_prev - m_next) * acc_scr[i] + jnp.dot(va, p, preferred_element_type=F32)
        m_scr[i] = m_next

    is_ctx = qi * tq >= n_lat
    n_chunks = t // tk

    @pl.when(jnp.logical_not(is_ctx))
    def _():
        scores(0, tk, 0, 0)

        def body(c, carry):
            off = pl.multiple_of(c * tk, tk)
            off_next = pl.multiple_of(jnp.minimum(c + 1, n_chunks - 1) * tk, tk)
            for i in range(n_maps):
                if i + 1 < n_maps:
                    scores(off, tk, i + 1, (i + 1) % 2)
                else:
                    scores(off_next, tk, 0, 0)
                accumulate(off, tk, i, i % 2)
            return carry

        lax.fori_loop(0, n_chunks, body, 0)

    @pl.when(is_ctx)
    def _():
        for i in range(n_maps):
            scores(n_lat, t - n_lat, i, i % 2)
            accumulate(n_lat, t - n_lat, i, i % 2)

    lam = lam_ref[0]
    outs = []
    for h in range(DA_HEADS):
        a0, a1 = acc_scr[2 * h], acc_scr[2 * h + 1]
        o = (a0[0:DA_V_DIM] / a0[DA_V_DIM:DA_V_DIM + 1]
             - lam * (a1[0:DA_V_DIM] / a1[DA_V_DIM:DA_V_DIM + 1]))
        inv = lax.rsqrt(jnp.mean(o * o, axis=0, keepdims=True) + NORM_EPS)
        outs.append(o * inv * gain_ref[...] * fin_scale)
    o_ref[...] = jnp.concatenate(outs, axis=0).T.astype(BF16)


def _diff_attention(z, qt, vta, lam, gain, n_lat, fin_scale):
    b, t, _ = z.shape
    tq = ATT_BLOCK
    tk = max(w for w in range(ATT_BLOCK, DA_MAX_KEY_CHUNK + 1, ATT_BLOCK) if t % w == 0)
    n_maps = 2 * DA_HEADS
    va_rows = vta.shape[2]
    return pl.pallas_call(
        functools.partial(_diff_attn_kernel, n_lat=n_lat, t=t, tq=tq, tk=tk, fin_scale=fin_scale),
        grid=(b, t // tq),
        in_specs=[pl.BlockSpec(memory_space=pltpu.SMEM),
                  pl.BlockSpec((None, GROUP, tq), lambda bi, m: (bi, 0, m)),
                  pl.BlockSpec((None, t, GROUP), lambda bi, m: (bi, 0, 1)),
                  pl.BlockSpec((None, DA_HEADS, va_rows, t), lambda bi, m: (bi, 0, 0, 0)),
                  pl.BlockSpec((DA_V_DIM, tq), lambda bi, m: (0, 0))],
        out_specs=pl.BlockSpec((None, tq, GROUP), lambda bi, m: (bi, m, 0)),
        out_shape=jax.ShapeDtypeStruct((b, t, GROUP), BF16),
        scratch_shapes=[pltpu.VMEM((n_maps, GROUP, tq), BF16), pltpu.VMEM((2, tk, tq), F32),
                        pltpu.VMEM((2, 1, tq), F32), pltpu.VMEM((n_maps, 1, tq), F32),
                        pltpu.VMEM((n_maps, va_rows, tq), F32)],
        compiler_params=_cparams(2),
    )(lam, qt, z, vta, gain)


def _retention_kernel(q_ref, k_ref, v_ref, g_ref, dm_ref, qw_ref, kw_ref, cd_ref, bd_ref, o_ref,
                      st_scr, of_scr, *, c, n_chunks, n_lat_chunks, nb):
    d = pl.program_id(0)
    i = pl.program_id(1)
    chunk = jnp.where(d == 0, (i + n_lat_chunks) % n_chunks, n_chunks - 1 - i)

    @pl.when(i == 0)
    def _():
        st_scr[...] = jnp.zeros(st_scr.shape, F32)

    lane_head = lax.broadcasted_iota(jnp.int32, (1, GROUP), 1) // RET_DIM
    off = pl.multiple_of(chunk * c, c)
    for bb in range(nb):
        q, k, v = q_ref[bb], k_ref[bb], v_ref[bb]
        o = jnp.dot((q.astype(F32) * qw_ref[...]).astype(BF16), st_scr[bb].astype(BF16),
                    preferred_element_type=F32)
        for h in range(RET_HEADS):
            mh = lane_head == h
            qh = q * mh.astype(BF16)
            s = lax.dot_general(qh, k, (((1,), (1,)), ((), ())), preferred_element_type=F32) * dm_ref[h]
            o = o + jnp.where(mh, jnp.dot(s.astype(BF16), v, preferred_element_type=F32), 0.0)
        kwk = (k.astype(F32) * kw_ref[...]).astype(BF16)
        kv = lax.dot_general(kwk, v, (((0,), (0,)), ((), ())), preferred_element_type=F32)
        st_scr[bb] = cd_ref[...] * st_scr[bb] + bd_ref[...] * kv

        @pl.when(d == 0)
        def _():
            of_scr[bb, pl.ds(off, c), :] = o

        @pl.when(d == 1)
        def _():
            tot = o + of_scr[bb, pl.ds(off, c), :]
            sq = tot * tot
            inv = jnp.zeros(tot.shape, F32)
            for h in range(RET_HEADS):
                mh = lane_head == h
                ms = jnp.sum(jnp.where(mh, sq, 0.0), axis=-1, keepdims=True) * (1.0 / RET_DIM)
                inv = jnp.where(mh, lax.rsqrt(ms + NORM_EPS), inv)
            o_ref[bb] = (tot * inv * jax.nn.silu(g_ref[bb].astype(F32))).astype(BF16)


def _retention_tables(decay_logit, c):
    log_g = jax.nn.log_sigmoid(decay_logit.astype(F32))
    pos = jnp.arange(c, dtype=F32)
    rel = pos[:, None] - pos[None, :]
    rel = jnp.stack([rel, -rel])
    dm = jnp.where(rel[:, None] >= 0, jnp.exp(log_g[:, :, None, None] * jnp.maximum(rel[:, None], 0.0)), 0.0)
    lg_lane = jnp.repeat(log_g, RET_DIM, axis=1)
    q_exp = jnp.stack([pos + 1.0, c - pos])
    k_exp = jnp.stack([c - 1.0 - pos, pos])
    qw = jnp.exp(lg_lane[:, None, :] * q_exp[:, :, None])
    kw = jnp.exp(lg_lane[:, None, :] * k_exp[:, :, None])
    head = np.arange(GROUP) // RET_DIM
    bd = jnp.asarray((head[:, None] == head[None, :]).astype(np.float32))
    cd = jnp.exp(lg_lane * c)[:, :, None] * jnp.ones((1, 1, GROUP), F32)
    return dm, qw, kw, cd, bd


def _retention(z, tables, layer, n_lat):
    b, t, _ = z.shape
    c = ATT_BLOCK
    n_chunks, n_lat_chunks = t // c, n_lat // c
    dm, qw, kw, cd, bd = tables

    def chunk_of(d, i):
        return jnp.where(d == 0, (i + n_lat_chunks) % n_chunks, n_chunks - 1 - i)

    def zspec(col):
        return pl.BlockSpec((b, c, GROUP), lambda d, i: (0, chunk_of(d, i), col))

    return pl.pallas_call(
        functools.partial(_retention_kernel, c=c, n_chunks=n_chunks, n_lat_chunks=n_lat_chunks, nb=b),
        grid=(2, n_chunks),
        in_specs=[zspec(3), zspec(4), zspec(5), zspec(6),
                  pl.BlockSpec((None, None, RET_HEADS, c, c), lambda d, i: (layer, d, 0, 0, 0)),
                  pl.BlockSpec((None, None, c, GROUP), lambda d, i: (layer, d, 0, 0)),
                  pl.BlockSpec((None, None, c, GROUP), lambda d, i: (layer, d, 0, 0)),
                  pl.BlockSpec((None, None, GROUP, GROUP), lambda d, i: (layer, d, 0, 0)),
                  pl.BlockSpec((None, GROUP, GROUP), lambda d, i: (layer, 0, 0))],
        out_specs=pl.BlockSpec((b, c, GROUP),
                               lambda d, i: (0, jnp.where(d == 0, n_chunks - 1, n_chunks - 1 - i), 0)),
        out_shape=jax.ShapeDtypeStruct((b, t, GROUP), BF16),
        scratch_shapes=[pltpu.VMEM((b, GROUP, GROUP), F32), pltpu.VMEM((b, t, GROUP), F32)],
        compiler_params=_cparams(2),
    )(z, z, z, z, dm, qw, kw, cd, bd)


def _nbr_attn_kernel(q_ref, kp_ref, kc_ref, kn_ref, kx_ref, vp_ref, vc_ref, vn_ref, vx_ref, tab_ref, o_ref):
    q = q_ref[...]
    k = jnp.concatenate([kp_ref[...], kc_ref[...], kn_ref[...], kx_ref[...]], axis=0)
    v = jnp.concatenate([vp_ref[...], vc_ref[...], vn_ref[...], vx_ref[...]], axis=0)
    lane_head = lax.broadcasted_iota(jnp.int32, (1, GROUP), 1) // NA_DIM
    o = jnp.zeros(q.shape, F32)
    for h in range(NA_HEADS):
        mh = lane_head == h
        qh = q * mh.astype(BF16)
        s = lax.dot_general(qh, k, (((1,), (1,)), ((), ())), preferred_element_type=F32) + tab_ref[h]
        p = jnp.exp(s - jnp.max(s, axis=-1, keepdims=True))
        l = jnp.sum(p, axis=-1, keepdims=True)
        o = o + jnp.where(mh, jnp.dot(p.astype(BF16), v, preferred_element_type=F32) / l, 0.0)
    o_ref[...] = o.astype(BF16)


def _nbr_tables(rpb, n_lat, n_ctx):
    blk_rows = ATT_BLOCK // GRID_W
    rows = n_lat // GRID_W
    n_blk = rows // blk_rows
    assert rows >= NA_WIN_ROWS and n_blk >= 3
    qcol = kcol = np.arange(GRID_W)
    c0 = np.clip(qcol - NA_WIN_COLS // 2, 0, GRID_W - NA_WIN_COLS)
    col_ok = (kcol[None, :] >= c0[:, None]) & (kcol[None, :] < c0[:, None] + NA_WIN_COLS)
    dc = np.clip(kcol[None, :] - qcol[:, None] + (NA_WIN_COLS - 1), 0, 2 * NA_WIN_COLS - 2)
    oh_c = np.eye(2 * NA_WIN_COLS - 1, dtype=np.float32)[dc]
    qr, kr = np.arange(blk_rows), np.arange(3 * blk_rows)
    oh_r, oks = [], []
    for m in (0, 1, n_blk - 1):
        qrow, krow = blk_rows * m + qr, blk_rows * (m - 1) + kr
        r0 = np.clip(qrow - NA_WIN_ROWS // 2, 0, rows - NA_WIN_ROWS)
        row_ok = (krow[None, :] >= r0[:, None]) & (krow[None, :] < r0[:, None] + NA_WIN_ROWS)
        dr = np.clip(krow[None, :] - qrow[:, None] + (NA_WIN_ROWS - 1), 0, 2 * NA_WIN_ROWS - 2)
        oh_r.append(np.eye(2 * NA_WIN_ROWS - 1, dtype=np.float32)[dr])
        oks.append((row_ok[:, None, :, None] & col_ok[None, :, None, :]).reshape(ATT_BLOCK, 3 * ATT_BLOCK))
    bias = jnp.einsum('vqka,hab,QKb->vhqQkK', jnp.asarray(np.stack(oh_r)), rpb.astype(F32), jnp.asarray(oh_c),
                      precision=lax.Precision.HIGHEST).reshape(3, NA_HEADS, ATT_BLOCK, 3 * ATT_BLOCK)
    tab = jnp.where(jnp.asarray(np.stack(oks))[:, None], bias, NEG_BIG)
    tab = jnp.concatenate([tab, jnp.full((1, NA_HEADS, ATT_BLOCK, 3 * ATT_BLOCK), NEG_BIG, F32)], axis=0)
    return jnp.concatenate([tab, jnp.zeros((4, NA_HEADS, ATT_BLOCK, n_ctx), F32)], axis=-1)


def _nbr_attention(z, tab, layer, n_lat):
    b, t, _ = z.shape
    n_ctx = t - n_lat
    assert n_ctx == ATT_BLOCK and n_lat % ATT_BLOCK == 0
    n_blk = n_lat // ATT_BLOCK

    def kv_specs(col):
        mk = lambda f: pl.BlockSpec((None, ATT_BLOCK, GROUP), lambda bi, m: (bi, f(m), col))
        return [mk(lambda m: jnp.clip(m - 1, 0, n_blk - 1)), mk(lambda m: jnp.minimum(m, n_blk - 1)),
                mk(lambda m: jnp.clip(m + 1, 0, n_blk - 1)), mk(lambda m: n_blk)]

    def variant(m):
        return jnp.where(m == 0, 0, jnp.where(m == n_blk - 1, 2, jnp.where(m == n_blk, 3, 1)))

    win = 3 * ATT_BLOCK + n_ctx
    return pl.pallas_call(
        _nbr_attn_kernel,
        grid=(b, n_blk + 1),
        in_specs=[pl.BlockSpec((None, ATT_BLOCK, GROUP), lambda bi, m: (bi, m, 7))] + kv_specs(8) + kv_specs(9)
                 + [pl.BlockSpec((None, None, NA_HEADS, ATT_BLOCK, win),
                                 lambda bi, m: (layer, variant(m), 0, 0, 0))],
        out_specs=pl.BlockSpec((None, ATT_BLOCK, GROUP), lambda bi, m: (bi, m, 0)),
        out_shape=jax.ShapeDtypeStruct((b, t, GROUP), BF16),
        compiler_params=_cparams(2),
    )(*([z] * 9), tab)


def _s5_kernel(u_ref, x0_ref, kb_ref, sm_ref, rm_ref, ap_ref, y_ref, xf_ref, st_scr, sloc_scr, xin_scr,
               *, rb):
    d = pl.program_id(0)
    i = pl.program_id(2)

    @pl.when(i == 0)
    def _():
        st_scr[...] = x0_ref[...]

    u = u_ref[...]
    sloc_scr[...] = jnp.dot(u, sm_ref[...], preferred_element_type=F32)
    ar, ai = ap_ref[0:1, :], ap_ref[1:2, :]
    ns = S5_NSTATE

    def step(n, carry):
        xr, xi = carry
        r = jnp.where(d == 0, n, rb - 1 - n)
        xin_scr[pl.ds(r, 1), 0:ns] = xr
        xin_scr[pl.ds(r, 1), ns:2 * ns] = xi
        s = sloc_scr[pl.ds(r, 1), :]
        return ar * xr - ai * xi + s[:, 0:ns], ar * xi + ai * xr + s[:, ns:2 * ns]

    xr, xi = lax.fori_loop(0, rb, step, (st_scr[:, 0:ns], st_scr[:, ns:2 * ns]))
    st_scr[:, 0:ns] = xr
    st_scr[:, ns:2 * ns] = xi
    y_state = jnp.dot(xin_scr[...].astype(BF16), rm_ref[...], preferred_element_type=F32)
    for direction in range(2):
        @pl.when(d == direction)
        def _():
            for t_out in range(S5_CHUNK):
                srcs = range(0, t_out + 1) if direction == 0 else range(t_out, S5_CHUNK)
                acc = y_state[:, t_out * GROUP:(t_out + 1) * GROUP]
                for t_in in srcs:
                    acc = acc + jnp.dot(u[:, t_in * GROUP:(t_in + 1) * GROUP], kb_ref[abs(t_out - t_in)],
                                        preferred_element_type=F32)
                y_ref[:, t_out * GROUP:(t_out + 1) * GROUP] = acc
    xf_ref[...] = st_scr[...]


def _s5_operators(a_re, a_im, b_re, b_im, c_re, c_im, log_step):
    tc, g, p, ch = S5_CHUNK, S5_GROUPS, S5_STATE, S5_CH
    hi = lax.Precision.HIGHEST
    ar, ai = a_re.astype(F32), a_im.astype(F32)
    dt = jnp.exp(log_step.astype(F32))[..., None]
    steps = jnp.arange(tc + 1, dtype=F32)[None, :, None, None]
    mag = jnp.exp((ar * dt)[:, None] * steps)
    ang = (ai * dt)[:, None] * steps
    pw_r, pw_i = mag * jnp.cos(ang), mag * jnp.sin(ang)
    x, y, den = pw_r[:, 1] - 1.0, pw_i[:, 1], ar * ar + ai * ai
    cf_r, cf_i = ((x * ar + y * ai) / den)[..., None], ((y * ar - x * ai) / den)[..., None]
    bre, bim = b_re.astype(F32), b_im.astype(F32)
    bb_r, bb_i = cf_r * bre - cf_i * bim, cf_r * bim + cf_i * bre
    cr, ci = c_re.astype(F32), c_im.astype(F32)
    w_r = pw_r[..., None] * bb_r[:, None] - pw_i[..., None] * bb_i[:, None]
    w_i = pw_r[..., None] * bb_i[:, None] + pw_i[..., None] * bb_r[:, None]
    kern = (jnp.einsum('dgcp,dtgph->dtghc', cr, w_r, precision=hi)
            - jnp.einsum('dgcp,dtgph->dtghc', ci, w_i, precision=hi))
    eye = jnp.eye(g, dtype=BF16)
    idx = np.arange(tc)
    kb = (kern[:, 0:tc, :, :, None, :].astype(BF16)
          * eye[None, None, :, None, :, None]).reshape(2, tc, g * ch, g * ch)
    outs = []
    for d in range(2):
        f_exp = jnp.asarray((tc - 1 - idx) if d == 0 else idx)
        sm = jnp.concatenate(
            [(jnp.transpose(part[d][f_exp], (0, 1, 3, 2)).astype(BF16)[:, :, :, None, :]
              * eye[None, :, None, :, None]).reshape(tc * g * ch, g * p) for part in (w_r, w_i)], axis=1)
        e_exp = jnp.asarray((idx + 1) if d == 0 else (tc - idx))
        pr, pi_ = pw_r[d][e_exp][:, :, None, :], pw_i[d][e_exp][:, :, None, :]
        ca_r, ca_i = cr[d][None] * pr - ci[d][None] * pi_, cr[d][None] * pi_ + ci[d][None] * pr
        rm = jnp.concatenate(
            [(jnp.transpose(part, (1, 3, 0, 2)).astype(BF16)[:, :, :, None, :]
              * eye[:, None, None, :, None]).reshape(g * p, tc * g * ch) for part in (ca_r, -ca_i)], axis=0)
        ap = jnp.stack([pw_r[d, tc].reshape(-1), pw_i[d, tc].reshape(-1)])
        outs.append((sm, rm, ap))
    sm, rm, ap = (jnp.stack(v) for v in zip(*outs))
    return kb, sm, rm, ap


def _s5_scan(u_rows, row0, r, x0, ops, layer, rb):
    b = u_rows.shape[0]
    nb = r // rb
    assert r % rb == 0 and row0 % rb == 0
    kb, sm, rm, ap = ops
    blk = lambda d, i: jnp.where(d == 0, i, nb - 1 - i)
    op_spec = lambda shp: pl.BlockSpec((None, None) + shp, lambda d, bi, i: (layer, d) + (0,) * len(shp),
                                       pipeline_mode=pl.Buffered(1))
    return pl.pallas_call(
        functools.partial(_s5_kernel, rb=rb),
        grid=(2, b, nb),
        in_specs=[pl.BlockSpec((None, rb, S5_ROW), lambda d, bi, i: (bi, row0 // rb + blk(d, i), 0)),
                  pl.BlockSpec((None, None, 1, 2 * S5_NSTATE), lambda d, bi, i: (d, bi, 0, 0)),
                  op_spec((S5_CHUNK, GROUP, GROUP)), op_spec((S5_ROW, 2 * S5_NSTATE)),
                  op_spec((2 * S5_NSTATE, S5_ROW)),
                  pl.BlockSpec((None, None, 2, S5_NSTATE), lambda d, bi, i: (layer, d, 0, 0))],
        out_specs=[pl.BlockSpec((None, None, rb, S5_ROW), lambda d, bi, i: (d, bi, blk(d, i), 0)),
                   pl.BlockSpec((None, None, 1, 2 * S5_NSTATE), lambda d, bi, i: (d, bi, 0, 0))],
        out_shape=[jax.ShapeDtypeStruct((2, b, r, S5_ROW), F32),
                   jax.ShapeDtypeStruct((2, b, 1, 2 * S5_NSTATE), F32)],
        scratch_shapes=[pltpu.VMEM((1, 2 * S5_NSTATE), F32), pltpu.VMEM((rb, 2 * S5_NSTATE), F32),
                        pltpu.VMEM((rb, 2 * S5_NSTATE), F32)],
        compiler_params=_cparams(3),
    )(u_rows, x0, kb, sm, rm, ap)


def _s5_glu_kernel(y_ref, u_ref, d_ref, w_ref, b_ref, o_ref, y_scr, *, rb):
    yf = y_ref[0] + y_ref[1]
    for r in range(S5_CHUNK):
        for hf in range(GROUP // 128):
            lo = r * GROUP + hf * 128
            y_scr[hf, pl.ds(r, rb, stride=S5_CHUNK), :] = yf[:, lo:lo + 128]
    y = (u_ref[...].astype(F32) * d_ref[...]
         + jnp.concatenate([y_scr[hf] for hf in range(GROUP // 128)], axis=-1))
    act = jax.nn.gelu(y)
    gate = jnp.dot(act.astype(BF16), w_ref[...], preferred_element_type=F32) + b_ref[...]
    o_ref[...] = (act * jax.nn.sigmoid(gate)).astype(BF16)


def _s5_glu(y, z, tok0, rb, d_skip, glu_w, glu_b):
    b, r = y.shape[1], y.shape[2]
    tm = rb * S5_CHUNK
    assert r % rb == 0 and tok0 % tm == 0
    return pl.pallas_call(
        functools.partial(_s5_glu_kernel, rb=rb),
        grid=(b, r // rb),
        in_specs=[pl.BlockSpec((2, None, rb, S5_ROW), lambda bi, m: (0, bi, m, 0)),
                  pl.BlockSpec((None, tm, GROUP), lambda bi, m: (bi, tok0 // tm + m, 10)),
                  pl.BlockSpec((1, GROUP), lambda bi, m: (0, 0)),
                  pl.BlockSpec((GROUP, GROUP), lambda bi, m: (0, 0)),
                  pl.BlockSpec((1, GROUP), lambda bi, m: (0, 0))],
        out_specs=pl.BlockSpec((None, tm, GROUP), lambda bi, m: (bi, m, 0)),
        out_shape=jax.ShapeDtypeStruct((b, r * S5_CHUNK, GROUP), BF16),
        scratch_shapes=[pltpu.VMEM((GROUP // 128, tm, 128), F32)],
        compiler_params=_cparams(2),
    )(y, z, d_skip, glu_w, glu_b)


def _s5_mixer(z, u_rows, ops, layer, d_skip, glu_w, glu_b, n_lat):
    b, t, _ = z.shape
    r_lat, r_ctx = n_lat // S5_CHUNK, (t - n_lat) // S5_CHUNK
    zero = jnp.zeros((2, b, 1, 2 * S5_NSTATE), F32)
    y_ctx, x_ctx = _s5_scan(u_rows, r_lat, r_ctx, zero, ops, layer, r_ctx)
    y_lat, _ = _s5_scan(u_rows, 0, r_lat, x_ctx, ops, layer, _pick_tile(r_lat, 256))
    glu = (d_skip.reshape(1, GROUP).astype(F32), glu_w.astype(BF16), glu_b.reshape(1, GROUP).astype(F32))
    s_lat = _s5_glu(y_lat, z, 0, _pick_tile(r_lat, 128), *glu)
    s_ctx = _s5_glu(y_ctx, z, n_lat, r_ctx, *glu)
    return jnp.concatenate([s_lat, s_ctx], axis=1)


def _out_proj_kernel(a_ref, r_ref, n_ref, s_ref, w_ref, x_ref, mod_ref, gain_ref, o_ref, *, n_lat, tm, d):
    m = pl.program_id(1)
    mix = jnp.dot(a_ref[...], w_ref[0:GROUP, :], preferred_element_type=F32)
    for j, ref in enumerate((r_ref, n_ref, s_ref), start=1):
        mix = mix + jnp.dot(ref[...], w_ref[j * GROUP:(j + 1) * GROUP, :], preferred_element_type=F32)
    gate = _row_mod(mod_ref, 2, m, tm, n_lat, d)
    o_ref[...] = x_ref[...] + gate * (_rms_rows(mix) * gain_ref[...])


def _out_proj(parts, w_bf, x, mod, gain, n_lat):
    b, t, d = x.shape
    tm = _pick_tile(t, 528)
    part_spec = pl.BlockSpec((None, tm, GROUP), lambda bi, m: (bi, m, 0))
    row_spec = pl.BlockSpec((None, tm, d), lambda bi, m: (bi, m, 0))
    return pl.pallas_call(
        functools.partial(_out_proj_kernel, n_lat=n_lat, tm=tm, d=d),
        grid=(b, t // tm),
        in_specs=[part_spec] * 4 + [pl.BlockSpec((d, d), lambda bi, m: (0, 0)), row_spec,
                                    pl.BlockSpec((None, 2, mod.shape[-1]), lambda bi, m: (bi, 0, 0)),
                                    pl.BlockSpec((1, d), lambda bi, m: (0, 0))],
        out_specs=row_spec,
        out_shape=jax.ShapeDtypeStruct((b, t, d), F32),
        compiler_params=_cparams(2),
    )(*parts, w_bf, x, mod, gain)


def _moe_norm_kernel(x_ref, mod_ref, gain_ref, router_ref, h_ref, route_ref, *, n_lat, tm, d):
    m = pl.program_id(1)
    shift = _row_mod(mod_ref, 3, m, tm, n_lat, d)
    scale = _row_mod(mod_ref, 4, m, tm, n_lat, d)
    h = (_rms_rows(x_ref[...]) * gain_ref[...]) * (1.0 + scale) + shift
    h_ref[...] = h.astype(BF16)
    logits = jnp.dot(h, router_ref[...], precision=lax.Precision.HIGHEST, preferred_element_type=F32)
    lane = lax.broadcasted_iota(jnp.int32, logits.shape, 1)
    logits = jnp.where(lane < MOE_EXPERTS, logits, -jnp.inf)
    v1 = jnp.max(logits, axis=-1, keepdims=True)
    i1 = jnp.min(jnp.where(logits == v1, lane, 128), axis=-1, keepdims=True)
    rest_l = jnp.where(lane == i1, -jnp.inf, logits)
    v2 = jnp.max(rest_l, axis=-1, keepdims=True)
    i2 = jnp.min(jnp.where(rest_l == v2, lane, 128), axis=-1, keepdims=True)
    e = jnp.exp(v2 - v1)
    w1 = 1.0 / (1.0 + e)
    w2 = e / (1.0 + e)
    route = jnp.where(lane == 0, w1, jnp.where(lane == 1, w2, jnp.where(
        lane == 2, i1.astype(F32), jnp.where(lane == 3, i2.astype(F32), 0.0))))
    route_ref[...] = route


def _moe_norm(x, mod, gain, router, n_lat):
    b, t, d = x.shape
    tm = _pick_tile(t, 528)
    row_spec = pl.BlockSpec((None, tm, d), lambda bi, m: (bi, m, 0))
    return pl.pallas_call(
        functools.partial(_moe_norm_kernel, n_lat=n_lat, tm=tm, d=d),
        grid=(b, t // tm),
        in_specs=[row_spec, pl.BlockSpec((None, 2, mod.shape[-1]), lambda bi, m: (bi, 0, 0)),
                  pl.BlockSpec((1, d), lambda bi, m: (0, 0)), pl.BlockSpec((d, 128), lambda bi, m: (0, 0))],
        out_specs=[row_spec, pl.BlockSpec((None, tm, 128), lambda bi, m: (bi, m, 0))],
        out_shape=[jax.ShapeDtypeStruct((b, t, d), BF16), jax.ShapeDtypeStruct((b, t, 128), F32)],
        compiler_params=_cparams(2),
    )(x, mod, gain, jnp.pad(router.astype(F32), ((0, 0), (0, 128 - MOE_EXPERTS))))


def _swiglu_kernel(te_ref, nu_ref, x_ref, wg_ref, wu_ref, wd_ref, o_ref, acc_scr):
    tile = pl.program_id(0)
    f = pl.program_id(1)
    nf = pl.num_programs(1)
    used = tile < nu_ref[0]

    @pl.when(f == 0)
    def _():
        acc_scr[...] = jnp.zeros(acc_scr.shape, F32)

    @pl.when(used)
    def _():
        x = x_ref[...]
        g = jnp.dot(x, wg_ref[...], preferred_element_type=F32)
        u = jnp.dot(x, wu_ref[...], preferred_element_type=F32)
        acc_scr[...] += jnp.dot((jax.nn.silu(g) * u).astype(BF16), wd_ref[...], preferred_element_type=F32)

    @pl.when(f == nf - 1)
    def _():
        o_ref[...] = acc_scr[...].astype(BF16)


def _swiglu_grouped(xs, tile_expert, n_used, w_gate, w_up, w_down, tm):
    p, d = xs.shape
    n_exp, _, f = w_gate.shape
    tf = f // 2
    grid_spec = pltpu.PrefetchScalarGridSpec(
        num_scalar_prefetch=2,
        grid=(p // tm, f // tf),
        in_specs=[pl.BlockSpec((tm, d), lambda t, j, te, nu: (t, 0)),
                  pl.BlockSpec((None, d, tf), lambda t, j, te, nu: (te[t], 0, j)),
                  pl.BlockSpec((None, d, tf), lambda t, j, te, nu: (te[t], 0, j)),
                  pl.BlockSpec((None, tf, d), lambda t, j, te, nu: (te[t], j, 0))],
        out_specs=pl.BlockSpec((tm, d), lambda t, j, te, nu: (t, 0)),
        scratch_shapes=[pltpu.VMEM((tm, d), F32)])
    return pl.pallas_call(
        _swiglu_kernel, grid_spec=grid_spec,
        out_shape=jax.ShapeDtypeStruct((p, d), BF16),
        compiler_params=_cparams(2),
    )(tile_expert, n_used, xs, w_gate, w_up, w_down)


def _moe_residual_kernel(y1_ref, y2_ref, route_ref, x_ref, mod_ref, gain_ref, o_ref, *, n_lat, tm, d):
    m = pl.program_id(1)
    y = route_ref[:, 0:1] * y1_ref[...].astype(F32) + route_ref[:, 1:2] * y2_ref[...].astype(F32)
    gate = _row_mod(mod_ref, 5, m, tm, n_lat, d)
    o_ref[...] = x_ref[...] + gate * (_rms_rows(y) * gain_ref[...])


def _moe_residual(y12, route, x, mod, gain, n_lat, rows):
    b, t, d = x.shape
    tm = _pick_tile(rows, 528)
    row_spec = pl.BlockSpec((None, tm, d), lambda bi, m: (bi, m, 0))
    return pl.pallas_call(
        functools.partial(_moe_residual_kernel, n_lat=n_lat, tm=tm, d=d),
        grid=(b, rows // tm),
        in_specs=[pl.BlockSpec((None, None, tm, d), lambda bi, m: (0, bi, m, 0)),
                  pl.BlockSpec((None, None, tm, d), lambda bi, m: (1, bi, m, 0)),
                  pl.BlockSpec((None, tm, 128), lambda bi, m: (bi, m, 0)), row_spec,
                  pl.BlockSpec((None, 2, mod.shape[-1]), lambda bi, m: (bi, 0, 0)),
                  pl.BlockSpec((1, d), lambda bi, m: (0, 0))],
        out_specs=row_spec,
        out_shape=jax.ShapeDtypeStruct((b, rows, d), F32),
        compiler_params=_cparams(2),
    )(y12, y12, route, x, mod, gain)


def _dense_ffn_kernel(x_ref, mod_ref, gin_ref, gout_ref, wg_ref, wu_ref, wd_ref, o_ref, h_scr, acc_scr,
                      *, n_lat, tm, d):
    m = pl.program_id(1)
    f = pl.program_id(2)

    @pl.when(f == 0)
    def _():
        shift = _row_mod(mod_ref, 3, m, tm, n_lat, d)
        scale = _row_mod(mod_ref, 4, m, tm, n_lat, d)
        h_scr[...] = ((_rms_rows(x_ref[...]) * gin_ref[...]) * (1.0 + scale) + shift).astype(BF16)
        acc_scr[...] = jnp.zeros(acc_scr.shape, F32)

    h = h_scr[...]
    g = jnp.dot(h, wg_ref[...], preferred_element_type=F32)
    u = jnp.dot(h, wu_ref[...], preferred_element_type=F32)
    acc_scr[...] += jnp.dot((jax.nn.silu(g) * u).astype(BF16), wd_ref[...], preferred_element_type=F32)

    @pl.when(f == pl.num_programs(2) - 1)
    def _():
        gate = _row_mod(mod_ref, 5, m, tm, n_lat, d)
        o_ref[...] = x_ref[...] + gate * (_rms_rows(acc_scr[...]) * gout_ref[...])


def _dense_ffn(x, mod, gain_in, gain_out, w_gate, w_up, w_down, n_lat):
    b, t, d = x.shape
    f = w_gate.shape[1]
    tm = _pick_tile(t, 528)
    tf = f // 2
    row_spec = pl.BlockSpec((None, tm, d), lambda bi, m, j: (bi, m, 0))
    vec_spec = pl.BlockSpec((1, d), lambda bi, m, j: (0, 0))
    return pl.pallas_call(
        functools.partial(_dense_ffn_kernel, n_lat=n_lat, tm=tm, d=d),
        grid=(b, t // tm, f // tf),
        in_specs=[row_spec, pl.BlockSpec((None, 2, mod.shape[-1]), lambda bi, m, j: (bi, 0, 0)),
                  vec_spec, vec_spec,
                  pl.BlockSpec((d, tf), lambda bi, m, j: (0, j)), pl.BlockSpec((d, tf), lambda bi, m, j: (0, j)),
                  pl.BlockSpec((tf, d), lambda bi, m, j: (j, 0))],
        out_specs=row_spec,
        out_shape=jax.ShapeDtypeStruct((b, t, d), F32),
        scratch_shapes=[pltpu.VMEM((tm, d), BF16), pltpu.VMEM((tm, d), F32)],
        compiler_params=_cparams(3),
    )(x, mod, gain_in, gain_out, w_gate, w_up, w_down)


def _moe_ffn(h, route, w_gate, w_up, w_down):
    b, t, d = h.shape
    n = b * t
    tm = _pick_tile(2 * n, 512)
    route = route.reshape(n, 128)
    expert = jnp.concatenate([route[:, 2], route[:, 3]]).astype(jnp.int32)
    token = jnp.concatenate([jnp.arange(n, dtype=jnp.int32)] * 2)
    onehot = (expert[:, None] == jnp.arange(MOE_EXPERTS, dtype=jnp.int32)[None, :]).astype(jnp.int32)
    counts = jnp.sum(onehot, axis=0)
    rank = jnp.sum((jnp.cumsum(onehot, axis=0) - onehot) * onehot, axis=1)
    padded = ((counts + tm - 1) // tm) * tm
    ends = jnp.cumsum(padded)
    pos = jnp.sum(onehot * (ends - padded)[None, :], axis=1) + rank
    n_rows = 2 * n + MOE_EXPERTS * tm
    n_tiles = n_rows // tm
    tile_start = jnp.arange(n_tiles, dtype=jnp.int32) * tm
    tile_expert = jnp.minimum(jnp.sum((ends[None, :] <= tile_start[:, None]).astype(jnp.int32), axis=1),
                              MOE_EXPERTS - 1)
    n_used = (ends[-1] // tm).astype(jnp.int32).reshape(1)
    row_token = jnp.zeros((n_rows,), jnp.int32).at[pos].set(token)
    xs = jnp.take(jnp.pad(h.reshape(n, d), ((0, n_rows - n), (0, 0))), row_token, axis=0)
    out = _swiglu_grouped(xs, tile_expert, n_used, w_gate, w_up, w_down, tm)
    return jnp.take(out, pos, axis=0).reshape(2, b, t, d)


def kernel(x, c, ctx, c_ctx, w_mod, b_mod, norm_gain, w_in, w_out, da_lambda, da_norm_gain,
           ret_decay_logit, na_rpb, s5_a_re, s5_a_im, s5_b_re, s5_b_im, s5_c_re, s5_c_im,
           s5_log_step, s5_d, s5_glu_w, s5_glu_b, ffn_w_gate, ffn_w_up, ffn_w_down,
           moe_router, moe_w_gate, moe_w_up, moe_w_down):
    b, n_lat, d = x.shape
    n_ctx = ctx.shape[1]
    t = n_lat + n_ctx
    depth = w_mod.shape[0]
    assert d == 4 * GROUP and b + 1 <= 8

    c_rows = jnp.zeros((8, d), F32).at[:b].set(c).at[b].set(c_ctx)
    mod_all = _modulation(c_rows, w_mod, b_mod)
    tabs = _rope_tables(n_lat, t, DA_QK_DIM) + _rope_tables(n_lat, t, RET_DIM)
    h_all = jnp.concatenate([x, ctx], axis=1)

    lam_inits = [0.8 - 0.6 * math.exp(-0.3 * layer) for layer in range(depth)]
    lp = da_lambda.astype(F32)
    lams = (jnp.exp(jnp.sum(lp[:, 0] * lp[:, 1], axis=-1)) - jnp.exp(jnp.sum(lp[:, 2] * lp[:, 3], axis=-1))
            + jnp.asarray(lam_inits, F32))
    ret_tabs = jax.vmap(lambda dl: _retention_tables(dl, ATT_BLOCK))(ret_decay_logit)
    na_tabs = jax.vmap(lambda r: _nbr_tables(r, n_lat, n_ctx))(na_rpb)
    s5_ops = jax.vmap(_s5_operators)(s5_a_re, s5_a_im, s5_b_re, s5_b_im, s5_c_re, s5_c_im, s5_log_step)

    for layer in range(depth):
        mod = jnp.stack([mod_all[layer, :b], jnp.broadcast_to(mod_all[layer, b], (b, 6 * d))], axis=1)
        g = norm_gain[layer].astype(F32)
        lam_init = lam_inits[layer]

        z, qt, vta, u_rows = _in_proj(h_all, mod, g[0:1], w_in[layer].astype(BF16), tabs, n_lat)
        da_gain = jnp.broadcast_to(da_norm_gain[layer].astype(F32)[:, None], (DA_V_DIM, ATT_BLOCK))
        a_out = _diff_attention(z, qt, vta, lams[layer].reshape(1), da_gain, n_lat, 1.0 - lam_init)
        r_out = _retention(z, ret_tabs, layer, n_lat)
        n_out = _nbr_attention(z, na_tabs, layer, n_lat)
        s_out = _s5_mixer(z, u_rows, s5_ops, layer, s5_d[layer], s5_glu_w[layer], s5_glu_b[layer], n_lat)
        h_all = _out_proj((a_out, r_out, n_out, s_out), w_out[layer].astype(BF16), h_all, mod, g[1:2], n_lat)

        i = layer // 2
        if layer % 2 == 0:
            h_all = _dense_ffn(h_all, mod, g[2:3], g[3:4], ffn_w_gate[i].astype(BF16), ffn_w_up[i].astype(BF16),
                               ffn_w_down[i].astype(BF16), n_lat)
        else:
            f_in, route = _moe_norm(h_all, mod, g[2:3], moe_router[i], n_lat)
            y12 = _moe_ffn(f_in, route, moe_w_gate[i].astype(BF16), moe_w_up[i].astype(BF16),
                           moe_w_down[i].astype(BF16))
            h_all = _moe_residual(y12, route, h_all, mod, g[3:4], n_lat, n_lat if layer == depth - 1 else t)
    return h_all[:, :n_lat]
```
